```python
import jax, jax.numpy as jnp
from jax import lax
import numpy as np

D_MODEL = 1024
BATCH = 1
SEQ = 16384
DEPTH = 2
DEC_BATCH = 32
DEC_SEQ = 1
PAST_LEN = 16384
PAGE_SIZE = 128

ATTN_HEADS = 8
HEAD_DIM = 64
ATTN_W = ATTN_HEADS * HEAD_DIM
ROT_DIM = HEAD_DIM // 4
ROPE_THETA = 500000.0
MOBA_BLOCK = 256
MOBA_TOPK = 3
QUERY_ROWS = 128
CONV_GROUPS = 8
CONV_W = D_MODEL - ATTN_W
CONV_WIDTH = 3
GMLP_W = D_MODEL
GMLP_GROUPS = 4
GMLP_CHUNK = 128
MEM_LEN = 256
MEM_HEADS = 4
MEM_HEAD_DIM = D_MODEL // MEM_HEADS
FFN_W = 4 * D_MODEL
N_EVEN = (DEPTH + 1) // 2
N_ODD = DEPTH // 2
EPS = 1e-6

kernel_name = 'moba_shortconv_gmlp_hybrid_step'


def rmsnorm(x, g):
    xf = x.astype(jnp.float32)
    y = xf * lax.rsqrt(jnp.mean(xf * xf, axis=-1, keepdims=True) + EPS)
    return (y * g.astype(jnp.float32)).astype(x.dtype)


def layernorm(x, g, b):
    xf = x.astype(jnp.float32)
    mu = jnp.mean(xf, axis=-1, keepdims=True)
    xc = xf - mu
    var = jnp.mean(xc * xc, axis=-1, keepdims=True)
    y = xc * lax.rsqrt(var + EPS) * g.astype(jnp.float32) + b.astype(jnp.float32)
    return y.astype(x.dtype)


def rope_partial(x, pos):
    half = ROT_DIM // 2
    inv = jnp.power(jnp.float32(ROPE_THETA), -jnp.arange(half, dtype=jnp.float32) * (2.0 / ROT_DIM))
    ang = pos.astype(jnp.float32)[:, None] * inv[None, :]
    cos = jnp.cos(ang)[None, :, None, :]
    sin = jnp.sin(ang)[None, :, None, :]
    xf = x.astype(jnp.float32)
    x1 = xf[..., :half]
    x2 = xf[..., half:ROT_DIM]
    out = jnp.concatenate([x1 * cos - x2 * sin, x2 * cos + x1 * sin, xf[..., ROT_DIM:]], axis=-1)
    return out.astype(x.dtype)


def to_blocks(pieces):
    b, _, h, d = pieces[0].shape
    l = sum(p.shape[1] for p in pieces)
    nb = max(-(-l // MOBA_BLOCK), MOBA_TOPK)
    pad = jnp.zeros((b, nb * MOBA_BLOCK - l, h, d), pieces[0].dtype)
    return jnp.concatenate(list(pieces) + [pad], axis=1).reshape(b, nb, MOBA_BLOCK, h, d)


def moba_block_attend(q, q_pos, k_blk, v_blk, k_mean):
    b, t, h, d = q.shape
    nb = k_blk.shape[1]
    qf = q.astype(jnp.float32)
    own = q_pos // MOBA_BLOCK
    gate = jnp.einsum('bthd,bnhd->bthn', qf, k_mean)
    fully_past = jnp.arange(nb)[None, :] < own[:, None]
    gate = jnp.where(fully_past[None, :, None, :], gate, -jnp.inf)
    _, sel = lax.top_k(gate, MOBA_TOPK)
    own_b = jnp.broadcast_to(own[None, :, None, None], (b, t, h, 1))
    blocks = jnp.concatenate([sel, own_b], axis=-1)
    keep = jnp.concatenate([sel < own_b, jnp.ones((b, t, h, 1), dtype=bool)], axis=-1)
    bi = jnp.arange(b)[:, None, None, None]
    hi = jnp.arange(h)[None, None, :, None]
    k_sel = k_blk[bi, blocks, :, hi]
    v_sel = v_blk[bi, blocks, :, hi]
    k_pos = blocks[..., None] * MOBA_BLOCK + jnp.arange(MOBA_BLOCK)
    mask = keep[..., None] & (k_pos <= q_pos[None, :, None, None, None])
    s = jnp.einsum('bthd,bthnjd->bthnj', qf, k_sel.astype(jnp.float32)) * (d ** -0.5)
    s = jnp.where(mask, s, -jnp.inf).reshape(b, t, h, -1)
    p = jax.nn.softmax(s, axis=-1).reshape(mask.shape)
    o = jnp.einsum('bthnj,bthnjd->bthd', p, v_sel.astype(jnp.float32))
    return o.astype(q.dtype)


def moba_attention(q, q_pos, k_pieces, v_pieces):
    k_blk = to_blocks(k_pieces)
    v_blk = to_blocks(v_pieces)
    k_mean = jnp.mean(k_blk.astype(jnp.float32), axis=2)
    b, t, h, d = q.shape
    qb = max(1, min(t, QUERY_ROWS // b))
    n = -(-t // qb)
    pad = n * qb - t
    qp = jnp.pad(q, ((0, 0), (0, pad), (0, 0), (0, 0)))
    pp = jnp.pad(q_pos, (0, pad), mode='edge')
    qc = qp.reshape(b, n, qb, h, d).transpose(1, 0, 2, 3, 4)
    pc = pp.reshape(n, qb)
    out = lax.map(lambda a: moba_block_attend(a[0], a[1], k_blk, v_blk, k_mean), (qc, pc))
    return out.transpose(1, 0, 2, 3, 4).reshape(b, n * qb, h, d)[:, :t]


def short_conv(u_past, u, w):
    cat = jnp.concatenate([u_past, u.astype(u_past.dtype)], axis=1)
    t = u.shape[1]
    y = cat[:, 0:t] * w[0]
    for i in range(1, CONV_WIDTH):
        y = y + cat[:, i:i + t] * w[i]
    return y, cat[:, cat.shape[1] - (CONV_WIDTH - 1):]


def even_mixer(h, pos, k_past, v_past, conv_past, w_in, g_q, g_k, w_conv, w_out):
    b, t, _ = h.shape
    z = h @ w_in
    q, k, v, bg, cg, xc = jnp.split(
        z, [ATTN_W, 2 * ATTN_W, 3 * ATTN_W, 3 * ATTN_W + CONV_W, 3 * ATTN_W + 2 * CONV_W], axis=-1)
    q = rope_partial(rmsnorm(q.reshape(b, t, ATTN_HEADS, HEAD_DIM), g_q), pos)
    k = rope_partial(rmsnorm(k.reshape(b, t, ATTN_HEADS, HEAD_DIM), g_k), pos)
    v = v.reshape(b, t, ATTN_HEADS, HEAD_DIM)
    if k_past is None:
        k_pieces, v_pieces = [k], [v]
    else:
        k_pieces = [k_past, k.astype(k_past.dtype)]
        v_pieces = [v_past, v.astype(v_past.dtype)]
    attn = moba_attention(q, pos, k_pieces, v_pieces).reshape(b, t, ATTN_W)
    conv_out, conv_state = short_conv(conv_past, cg * xc, w_conv)
    y = jnp.concatenate([attn, (bg * conv_out).astype(attn.dtype)], axis=-1) @ w_out
    return y, k, v, conv_state


def chunk_gmlp(h, w_in, ln_g, ln_b, w_s, b_s, w_out):
    b, t, _ = h.shape
    z = jax.nn.gelu(h @ w_in)
    u, v = jnp.split(z, 2, axis=-1)
    v = layernorm(v, ln_g, ln_b)
    n = -(-t // GMLP_CHUNK)
    pad = n * GMLP_CHUNK - t
    dg = GMLP_W // GMLP_GROUPS
    vc = jnp.pad(v, ((0, 0), (0, pad), (0, 0))).reshape(b, n, GMLP_CHUNK, GMLP_GROUPS, dg)
    ws = w_s * jnp.tril(jnp.ones((GMLP_CHUNK, GMLP_CHUNK), w_s.dtype))
    s = jnp.einsum('gij,bcjgd->bcigd', ws, vc) + b_s.T[None, None, :, :, None]
    s = s.reshape(b, n * GMLP_CHUNK, GMLP_W)[:, :t]
    return (u * s) @ w_out, v


def mem_kv(mem, g_mem, w_k, w_v, g_k):
    b, m, _ = mem.shape
    mm = rmsnorm(mem, g_mem)
    k = rmsnorm((mm @ w_k).reshape(b, m, MEM_HEADS, MEM_HEAD_DIM), g_k)
    v = (mm @ w_v).reshape(b, m, MEM_HEADS, MEM_HEAD_DIM)
    return k, v


def mem_attend(h, k, v, w_q, g_q, w_o):
    b, t, _ = h.shape
    q = rmsnorm((h @ w_q).reshape(b, t, MEM_HEADS, MEM_HEAD_DIM), g_q)
    s = jnp.einsum('bthd,bmhd->bhtm', q.astype(jnp.float32), k.astype(jnp.float32)) * (MEM_HEAD_DIM ** -0.5)
    p = jax.nn.softmax(s, axis=-1)
    o = jnp.einsum('bhtm,bmhd->bthd', p, v.astype(jnp.float32)).astype(h.dtype)
    return o.reshape(b, t, D_MODEL) @ w_o


def sq_relu_ffn(h, w1, w2):
    a = jax.nn.relu(h @ w1)
    return (a * a) @ w2


def setup_inputs(seed: int = 0) -> dict:
    key = jax.random.key(seed)
    ks = jax.random.split(key, 40)
    n_pages = PAST_LEN // PAGE_SIZE
    n_used = DEC_BATCH * n_pages
    n_pool = n_used + n_used // 4

    def nrm(k, shape, scale=1.0):
        return jax.random.normal(k, shape, jnp.float32) * scale

    def gain(k, shape):
        return 1.0 + 0.05 * nrm(k, shape)

    in_w = 3 * ATTN_W + 3 * CONV_W
    page_table = jax.random.permutation(ks[9], n_pool)[:n_used].reshape(DEC_BATCH, n_pages).astype(jnp.int32)
    return {
        'x_prompt': nrm(ks[0], (BATCH, SEQ, D_MODEL)),
        'x_sample': nrm(ks[1], (DEC_BATCH, DEC_SEQ, D_MODEL)),
        'mem_prompt': nrm(ks[2], (BATCH, MEM_LEN, D_MODEL)),
        'cache_attn_k': nrm(ks[3], (N_EVEN, n_pool, PAGE_SIZE, ATTN_HEADS, HEAD_DIM)),
        'cache_attn_v': nrm(ks[4], (N_EVEN, n_pool, PAGE_SIZE, ATTN_HEADS, HEAD_DIM)),
        'state_conv': nrm(ks[5], (N_EVEN, DEC_BATCH, CONV_WIDTH - 1, CONV_W)),
        'cache_mem_k': nrm(ks[6], (DEPTH, DEC_BATCH, MEM_LEN, MEM_HEADS, MEM_HEAD_DIM)),
        'cache_mem_v': nrm(ks[7], (DEPTH, DEC_BATCH, MEM_LEN, MEM_HEADS, MEM_HEAD_DIM)),
        'page_table': page_table,
        'norm_mix': gain(ks[10], (DEPTH, D_MODEL)),
        'norm_cross': gain(ks[11], (DEPTH, D_MODEL)),
        'norm_mem': gain(ks[12], (DEPTH, D_MODEL)),
        'norm_ffn': gain(ks[13], (DEPTH, D_MODEL)),
        'mix_w_in': nrm(ks[14], (N_EVEN, D_MODEL, in_w), D_MODEL ** -0.5),
        'attn_g_q': gain(ks[15], (N_EVEN, HEAD_DIM)),
        'attn_g_k': gain(ks[16], (N_EVEN, HEAD_DIM)),
        'conv_w': nrm(ks[17], (N_EVEN, CONV_WIDTH, CONV_W), CONV_WIDTH ** -0.5),
        'mix_w_out': nrm(ks[18], (N_EVEN, ATTN_W + CONV_W, D_MODEL), (ATTN_W + CONV_W) ** -0.5),
        'gmlp_w_in': nrm(ks[19], (N_ODD, D_MODEL, 2 * GMLP_W), D_MODEL ** -0.5),
        'gmlp_ln_g': gain(ks[20], (N_ODD, GMLP_W)),
        'gmlp_ln_b': nrm(ks[21], (N_ODD, GMLP_W), 0.02),
        'gmlp_w_s': nrm(ks[22], (N_ODD, GMLP_GROUPS, GMLP_CHUNK, GMLP_CHUNK), GMLP_CHUNK ** -0.5),
        'gmlp_b_s': nrm(ks[23], (N_ODD, GMLP_GROUPS, GMLP_CHUNK), 0.02),
        'gmlp_w_out': nrm(ks[24], (N_ODD, GMLP_W, D_MODEL), GMLP_W ** -0.5),
        'cross_w_q': nrm(ks[25], (DEPTH, D_MODEL, D_MODEL), D_MODEL ** -0.5),
        'cross_w_k': nrm(ks[26], (DEPTH, D_MODEL, D_MODEL), D_MODEL ** -0.5),
        'cross_w_v': nrm(ks[27], (DEPTH, D_MODEL, D_MODEL), D_MODEL ** -0.5),
        'cross_w_o': nrm(ks[28], (DEPTH, D_MODEL, D_MODEL), D_MODEL ** -0.5),
        'cross_g_q': gain(ks[29], (DEPTH, MEM_HEAD_DIM)),
        'cross_g_k': gain(ks[30], (DEPTH, MEM_HEAD_DIM)),
        'ffn_w1': nrm(ks[31], (DEPTH, D_MODEL, FFN_W), D_MODEL ** -0.5),
        'ffn_w2': nrm(ks[32], (DEPTH, FFN_W, D_MODEL), FFN_W ** -0.5),
    }


def reference(x_prompt, x_sample, mem_prompt, cache_attn_k, cache_attn_v, state_conv,
              cache_mem_k, cache_mem_v, page_table, norm_mix, norm_cross, norm_mem, norm_ffn,
              mix_w_in, attn_g_q, attn_g_k, conv_w, mix_w_out, gmlp_w_in, gmlp_ln_g, gmlp_ln_b,
              gmlp_w_s, gmlp_b_s, gmlp_w_out, cross_w_q, cross_w_k, cross_w_v, cross_w_o,
              cross_g_q, cross_g_k, ffn_w1, ffn_w2):
    b_p, t_p, _ = x_prompt.shape
    b_s, t_s, _ = x_sample.shape
    pos_p = jnp.arange(t_p, dtype=jnp.int32)
    pos_s = PAST_LEN + jnp.arange(t_s, dtype=jnp.int32)
    hp, hs = x_prompt, x_sample
    ak_p, av_p, ak_s, av_s, cs_p, cs_s, gv_s, mk_p, mv_p = [], [], [], [], [], [], [], [], []
    for layer in range(DEPTH):
        li = layer // 2
        if layer % 2 == 0:
            conv0 = jnp.zeros((b_p, CONV_WIDTH - 1, CONV_W), hp.dtype)
            yp, kp, vp, cp = even_mixer(rmsnorm(hp, norm_mix[layer]), pos_p, None, None, conv0,
                                        mix_w_in[li], attn_g_q[li], attn_g_k[li], conv_w[li], mix_w_out[li])
            k_past = cache_attn_k[li, page_table].reshape(b_s, -1, ATTN_HEADS, HEAD_DIM)
            v_past = cache_attn_v[li, page_table].reshape(b_s, -1, ATTN_HEADS, HEAD_DIM)
            ys, ks_, vs_, cs = even_mixer(rmsnorm(hs, norm_mix[layer]), pos_s, k_past, v_past, state_conv[li],
                                          mix_w_in[li], attn_g_q[li], attn_g_k[li], conv_w[li], mix_w_out[li])
            ak_p.append(kp)
            av_p.append(vp)
            ak_s.append(ks_)
            av_s.append(vs_)
            cs_p.append(cp)
            cs_s.append(cs)
        else:
            yp, _ = chunk_gmlp(rmsnorm(hp, norm_mix[layer]), gmlp_w_in[li], gmlp_ln_g[li], gmlp_ln_b[li],
                               gmlp_w_s[li], gmlp_b_s[li], gmlp_w_out[li])
            ys, gv = chunk_gmlp(rmsnorm(hs, norm_mix[layer]), gmlp_w_in[li], gmlp_ln_g[li], gmlp_ln_b[li],
                                gmlp_w_s[li], gmlp_b_s[li], gmlp_w_out[li])
            gv_s.append(gv)
        hp = hp + yp
        hs = hs + ys
        mkp, mvp = mem_kv(mem_prompt, norm_mem[layer], cross_w_k[layer], cross_w_v[layer], cross_g_k[layer])
        mk_p.append(mkp)
        mv_p.append(mvp)
        hp = hp + mem_attend(rmsnorm(hp, norm_cross[layer]), mkp, mvp, cross_w_q[layer], cross_g_q[layer], cross_w_o[layer])
        hs = hs + mem_attend(rmsnorm(hs, norm_cross[layer]), cache_mem_k[layer], cache_mem_v[layer],
                             cross_w_q[layer], cross_g_q[layer], cross_w_o[layer])
        hp = hp + sq_relu_ffn(rmsnorm(hp, norm_ffn[layer]), ffn_w1[layer], ffn_w2[layer])
        hs = hs + sq_relu_ffn(rmsnorm(hs, norm_ffn[layer]), ffn_w1[layer], ffn_w2[layer])
    return (hp, hs, jnp.stack(ak_p), jnp.stack(av_p), jnp.stack(ak_s), jnp.stack(av_s),
            jnp.stack(cs_p), jnp.stack(cs_s), jnp.stack(gv_s), jnp.stack(mk_p), jnp.stack(mv_p))
```

```python
import functools

import jax
import jax.numpy as jnp
from jax import lax
from jax.experimental import pallas as pl
from jax.experimental.pallas import tpu as pltpu

D_MODEL = 1024
ATTN_HEADS = 8
HEAD_DIM = 64
ATTN_W = ATTN_HEADS * HEAD_DIM
ROT_DIM = HEAD_DIM // 4
ROPE_THETA = 500000.0
MOBA_BLOCK = 256
MOBA_TOPK = 3
PAGE_SIZE = 128
CONV_W = D_MODEL - ATTN_W
CONV_WIDTH = 3
GMLP_W = D_MODEL
GMLP_GROUPS = 4
GMLP_CHUNK = 128
MEM_HEADS = 4
MEM_HEAD_DIM = D_MODEL // MEM_HEADS
FFN_W = 4 * D_MODEL
EPS = 1e-6

MXU_DTYPE = jnp.bfloat16
V7X_LANES = 128
V7X_VMEM_BYTES = 64 * 1024 * 1024
VMEM_LIMIT = V7X_VMEM_BYTES * 7 // 8
NEG = -1e30
F32 = jnp.float32

ROW_TILE = 512
FFN_ROW_TILE = 1024
FFN_COL_TILE = 1024
PAGES_PER_BLOCK = MOBA_BLOCK // PAGE_SIZE
NT_DIMS = (((1,), (1,)), ((), ()))


def _cparams(*semantics):
    return pltpu.CompilerParams(dimension_semantics=semantics, vmem_limit_bytes=VMEM_LIMIT)


def _full(shape):
    n = len(shape)
    return pl.BlockSpec(shape, lambda *_: (0,) * n)


def _rows(tm, width):
    return pl.BlockSpec((tm, width), lambda i: (i, 0))


def _rms(x, g):
    return x * lax.rsqrt(jnp.mean(x * x, axis=-1, keepdims=True) + EPS) * g


def _mm(a, b):
    return jnp.dot(a, b, preferred_element_type=F32)


def _top_ids(g, lanef):
    ids = []
    for _ in range(MOBA_TOPK):
        mx = jnp.max(g, axis=1, keepdims=True)
        idx = jnp.min(jnp.where(g == mx, lanef, float(V7X_LANES)), axis=1, keepdims=True)
        ids.append(jnp.where(mx > 0.5 * NEG, idx, -1.0))
        g = jnp.where(lanef == idx, NEG, g)
    return ids


def _mix_in_body(seq_mode, tm, pos_base, pos_stride, *refs):
    if seq_mode:
        (x_ref, g_ref, w_ref, gq_ref, gk_ref, hm_ref, inv_ref, cw_ref,
         qs_ref, k_ref, v_ref, kb_ref, vb_ref, cm_ref, kmean_ref, tail_ref, ubuf) = refs
    else:
        (x_ref, g_ref, w_ref, gq_ref, gk_ref, hm_ref, inv_ref, cw_ref, p2_ref, p1_ref,
         qs_ref, k_ref, v_ref, cm_ref, u_ref) = refs
    i = pl.program_id(0)
    xn = _rms(x_ref[...], g_ref[...]).astype(MXU_DTYPE)

    def proj(c):
        return _mm(xn, w_ref[:, c * ATTN_W:(c + 1) * ATTN_W])

    hm = hm_ref[...]

    def head_rms(t, g):
        t2 = t * t
        hi = t2.astype(MXU_DTYPE)
        lo = (t2 - hi.astype(F32)).astype(MXU_DTYPE)
        ss = _mm(hi, hm) + _mm(lo, hm)
        return t * lax.rsqrt(ss * (1.0 / HEAD_DIM) + EPS) * g

    row = lax.broadcasted_iota(jnp.int32, (tm, V7X_LANES), 0)
    pos = (pos_base + pos_stride * (i * tm + row)).astype(F32)
    ang = pos * inv_ref[...]
    cs, sn = jnp.cos(ang), jnp.sin(ang)
    lane = lax.broadcasted_iota(jnp.int32, (tm, V7X_LANES), 1) & (HEAD_DIM - 1)
    half = ROT_DIM // 2
    reps = ATTN_W // V7X_LANES
    c_t = jnp.concatenate([jnp.where(lane < ROT_DIM, cs, 1.0)] * reps, axis=1)
    s_up = jnp.concatenate([jnp.where((lane >= half) & (lane < ROT_DIM), sn, 0.0)] * reps, axis=1)
    s_dn = jnp.concatenate([jnp.where(lane < half, -sn, 0.0)] * reps, axis=1)

    def rope(t):
        return t * c_t + pltpu.roll(t, half, 1) * s_up + pltpu.roll(t, ATTN_W - half, 1) * s_dn

    q = rope(head_rms(proj(0), gq_ref[...]))
    k = rope(head_rms(proj(1), gk_ref[...]))
    v = proj(2)
    qs_ref[...] = (q * (HEAD_DIM ** -0.5)).astype(qs_ref.dtype)
    k_ref[...] = k
    v_ref[...] = v

    bg = proj(3)
    u = proj(4) * proj(5)
    cw = cw_ref[...]
    if seq_mode:
        kb_ref[...] = k.astype(kb_ref.dtype)
        vb_ref[...] = v.astype(vb_ref.dtype)
        for b in range(tm // MOBA_BLOCK):
            kmean_ref[0, b:b + 1, :] = jnp.mean(k[b * MOBA_BLOCK:(b + 1) * MOBA_BLOCK], axis=0, keepdims=True)

        @pl.when(i == 0)
        def _():
            ubuf[0:8, :] = jnp.zeros((8, CONV_W), F32)

        ubuf[8:8 + tm, :] = u
        u1 = ubuf[7:7 + tm, :]
        u2 = ubuf[6:6 + tm, :]
        ubuf[0:8, :] = u[tm - 8:tm, :]
        tail_ref[...] = u[tm - 8:tm, :]
    else:
        u2 = p2_ref[...]
        u1 = p1_ref[...]
        u_ref[...] = u
    conv = u2 * cw[0:1, :] + u1 * cw[1:2, :] + u * cw[2:3, :]
    cm_ref[...] = (bg * conv).astype(cm_ref.dtype)


def _mix_in(x, gain, w_in, gq, gk, hmat, inv_lanes, conv_w, *, seq_mode, pos_base, pos_stride, prev=None):
    t = x.shape[0]
    tm = min(ROW_TILE, t)
    assert t % tm == 0 and (not seq_mode or tm % MOBA_BLOCK == 0)
    n = t // tm
    in_specs = [_rows(tm, D_MODEL), _full((1, D_MODEL)), _full(w_in.shape), _full((1, ATTN_W)),
                _full((1, ATTN_W)), _full(hmat.shape), _full((1, V7X_LANES)), _full((CONV_WIDTH, CONV_W))]
    args = [x, gain, w_in, gq, gk, hmat, inv_lanes, conv_w]
    f32o = jax.ShapeDtypeStruct((t, ATTN_W), F32)
    b16o = jax.ShapeDtypeStruct((t, ATTN_W), MXU_DTYPE)
    if seq_mode:
        nb = tm // MOBA_BLOCK
        out_shape = [b16o, f32o, f32o, b16o, b16o, b16o,
                     jax.ShapeDtypeStruct((n, nb, ATTN_W), F32), jax.ShapeDtypeStruct((8, CONV_W), F32)]
        out_specs = [_rows(tm, ATTN_W)] * 6 + [pl.BlockSpec((1, nb, ATTN_W), lambda i: (i, 0, 0)),
                                                _full((8, CONV_W))]
        scratch = [pltpu.VMEM((tm + 8, CONV_W), F32)]
    else:
        in_specs += [_rows(tm, CONV_W)] * 2
        args += list(prev)
        out_shape = [f32o, f32o, f32o, b16o, f32o]
        out_specs = [_rows(tm, ATTN_W)] * 5
        scratch = []
    return pl.pallas_call(
        functools.partial(_mix_in_body, seq_mode, tm, pos_base, pos_stride),
        grid=(n,), in_specs=in_specs, out_specs=out_specs, out_shape=out_shape, scratch_shapes=scratch,
        compiler_params=_cparams("arbitrary"), name="mix_in_seq" if seq_mode else "mix_in_rows")(*args)


def _moba_seq_body(q_ref, k_ref, v_ref, km_ref, o_ref, m_scr, l_scr, acc_scr, id_scr):
    i = pl.program_id(1)
    blk = MOBA_BLOCK
    q = q_ref[...]
    lane = lax.broadcasted_iota(jnp.int32, (blk, V7X_LANES), 1)
    lanef = lane.astype(F32)
    head_lanes = (lane < HEAD_DIM, lane >= HEAD_DIM)
    qh = [jnp.where(hl, q, jnp.zeros_like(q)) for hl in head_lanes]
    rowi = lax.broadcasted_iota(jnp.int32, (blk, blk), 0)
    coli = lax.broadcasted_iota(jnp.int32, (blk, blk), 1)
    own = pl.multiple_of(i * blk, blk)
    k_own = k_ref[pl.ds(own, blk), :]
    v_own = v_ref[pl.ds(own, blk), :]
    for h in range(2):
        gate = lax.dot_general(qh[h], km_ref[...], NT_DIMS, preferred_element_type=F32)
        ids = _top_ids(jnp.where(lane < i, gate, NEG), lanef)
        for r in range(MOBA_TOPK):
            id_scr[h, r] = jnp.broadcast_to(ids[r], (blk, V7X_LANES))
        s = lax.dot_general(qh[h], k_own, NT_DIMS, preferred_element_type=F32)
        s = jnp.where(coli <= rowi, s, NEG)
        m = jnp.max(s, axis=1, keepdims=True)
        p = jnp.exp(s - m)
        m_scr[h] = jnp.broadcast_to(m, (blk, V7X_LANES))
        l_scr[h] = jnp.broadcast_to(jnp.sum(p, axis=1, keepdims=True), (blk, V7X_LANES))
        acc_scr[h] = _mm(p.astype(MXU_DTYPE), v_own)

    def step(j, carry):
        start = pl.multiple_of(j * blk, blk)
        kj = k_ref[pl.ds(start, blk), :]
        vj = v_ref[pl.ds(start, blk), :]
        jf = j.astype(F32)
        for h in range(2):
            picked = (id_scr[h, 0] == jf) | (id_scr[h, 1] == jf) | (id_scr[h, 2] == jf)
            bias = jnp.where(picked, 0.0, NEG)
            s = lax.dot_general(qh[h], kj, NT_DIMS, preferred_element_type=F32)
            s = s + jnp.concatenate([bias, bias], axis=1)
            m_prev = m_scr[h]
            m_new = jnp.maximum(m_prev, jnp.max(s, axis=1, keepdims=True))
            alpha = jnp.exp(m_prev - m_new)
            p = jnp.exp(s - jnp.concatenate([m_new, m_new], axis=1))
            l_scr[h] = alpha * l_scr[h] + jnp.sum(p, axis=1, keepdims=True)
            acc_scr[h] = alpha * acc_scr[h] + _mm(p.astype(MXU_DTYPE), vj)
            m_scr[h] = m_new
        return carry

    lax.fori_loop(0, i, step, 0)
    o = jnp.where(head_lanes[0], acc_scr[0] / l_scr[0], acc_scr[1] / l_scr[1])
    o_ref[...] = o.astype(o_ref.dtype)


def _moba_seq(qs, kb, vb, kmean_pad):
    t = qs.shape[0]
    assert t % MOBA_BLOCK == 0 and t // MOBA_BLOCK <= V7X_LANES
    pairs = ATTN_W // V7X_LANES
    col = lambda p, i: (0, p)
    return pl.pallas_call(
        _moba_seq_body,
        grid=(pairs, t // MOBA_BLOCK),
        in_specs=[pl.BlockSpec((MOBA_BLOCK, V7X_LANES), lambda p, i: (i, p)),
                  pl.BlockSpec((t, V7X_LANES), col), pl.BlockSpec((t, V7X_LANES), col),
                  pl.BlockSpec((V7X_LANES, V7X_LANES), col)],
        out_specs=pl.BlockSpec((MOBA_BLOCK, V7X_LANES), lambda p, i: (i, p)),
        out_shape=jax.ShapeDtypeStruct((t, ATTN_W), MXU_DTYPE),
        scratch_shapes=[pltpu.VMEM((2, MOBA_BLOCK, V7X_LANES), F32)] * 3
        + [pltpu.VMEM((2, MOBA_TOPK, MOBA_BLOCK, V7X_LANES), F32)],
        compiler_params=_cparams("arbitrary", "arbitrary"), name="moba_seq")(qs, kb, vb, kmean_pad)


def _head_rows(q_row):
    sub = lax.broadcasted_iota(jnp.int32, (ATTN_HEADS, ATTN_W), 0)
    lane = lax.broadcasted_iota(jnp.int32, (ATTN_HEADS, ATTN_W), 1)
    return jnp.where(lane // HEAD_DIM == sub, jnp.broadcast_to(q_row, (ATTN_HEADS, ATTN_W)), 0.0)


def _moba_paged_scores_body(n_pages, pt_ref, q_ref, kn_ref, kp_ref, p_ref, ids_ref, pn_ref, gsum):
    del pt_ref
    pg = pl.program_id(1)
    qd = _head_rows(q_ref[0])
    s = lax.dot_general(qd.astype(MXU_DTYPE), kp_ref[0].astype(MXU_DTYPE), NT_DIMS,
                        preferred_element_type=F32)
    p_ref[0, pg] = s
    lane = lax.broadcasted_iota(jnp.int32, (ATTN_HEADS, V7X_LANES), 1)

    @pl.when(pg == 0)
    def _():
        gsum[...] = jnp.zeros_like(gsum)

    gsum[...] += jnp.where(lane == pg // PAGES_PER_BLOCK, jnp.sum(s, axis=1, keepdims=True), 0.0)

    @pl.when(pg == n_pages - 1)
    def _():
        n_blocks = n_pages // PAGES_PER_BLOCK
        lanef = lane.astype(F32)
        gate = jnp.where(lane < n_blocks, gsum[...] * (1.0 / MOBA_BLOCK), NEG)
        ids = _top_ids(gate, lanef)
        for r in range(MOBA_TOPK):
            ids_ref[0, r] = jnp.broadcast_to(ids[r], (ATTN_HEADS, V7X_LANES)).astype(jnp.int32)
        s_new = jnp.sum(qd * kn_ref[0], axis=1, keepdims=True)
        sc = p_ref[0]
        blk = (lax.broadcasted_iota(jnp.int32, sc.shape, 0) // PAGES_PER_BLOCK).astype(F32)
        picked = (blk == ids[0][None]) | (blk == ids[1][None]) | (blk == ids[2][None])
        sc = jnp.where(picked, sc, NEG)
        m = jnp.maximum(jnp.max(jnp.max(sc, axis=0), axis=1, keepdims=True), s_new)
        e = jnp.where(picked, jnp.exp(sc - m[None]), 0.0)
        e_new = jnp.exp(s_new - m)
        inv = 1.0 / (jnp.sum(jnp.sum(e, axis=0), axis=1, keepdims=True) + e_new)
        p_ref[0] = e * inv[None]
        pn_ref[0] = jnp.broadcast_to(e_new * inv, (ATTN_HEADS, V7X_LANES))


def _moba_paged_pv_body(pt_ref, ids_ref, p_ref, pn_ref, vn_ref, va_ref, vb_ref, o_ref, acc):
    del pt_ref
    b, slot = pl.program_id(0), pl.program_id(1)
    n_slots = ATTN_HEADS * MOBA_TOPK
    h = slot // MOBA_TOPK
    first = jnp.maximum(ids_ref[b * n_slots + slot], 0) * PAGES_PER_BLOCK
    sub = lax.broadcasted_iota(jnp.int32, (ATTN_HEADS, V7X_LANES), 0)

    @pl.when(slot == 0)
    def _():
        acc[...] = jnp.zeros_like(acc)

    for half, v_ref in enumerate((va_ref, vb_ref)):
        w = jnp.where(sub == h, p_ref[0, first + half], 0.0)
        acc[...] += _mm(w.astype(MXU_DTYPE), v_ref[0].astype(MXU_DTYPE))

    @pl.when(slot == n_slots - 1)
    def _():
        full = acc[...] + pn_ref[0][:, 0:1] * vn_ref[0]
        sub_w = lax.broadcasted_iota(jnp.int32, (ATTN_HEADS, ATTN_W), 0)
        lane_w = lax.broadcasted_iota(jnp.int32, (ATTN_HEADS, ATTN_W), 1)
        o_ref[0] = jnp.sum(jnp.where(lane_w // HEAD_DIM == sub_w, full, 0.0), axis=0,
                           keepdims=True).astype(o_ref.dtype)


def _moba_paged(qs, k_new, v_new, cache_k, cache_v, page_ids):
    nb, n_pages = page_ids.shape
    assert n_pages % PAGES_PER_BLOCK == 0 and n_pages // PAGES_PER_BLOCK <= V7X_LANES
    pt = page_ids.reshape(-1)
    row3 = lambda a: a.reshape(nb, 1, ATTN_W)
    per_b = lambda *shape: pl.BlockSpec((1,) + shape, lambda b, s, *_: (b,) + (0,) * len(shape))
    probs, ids, p_new = pl.pallas_call(
        functools.partial(_moba_paged_scores_body, n_pages),
        grid_spec=pltpu.PrefetchScalarGridSpec(
            num_scalar_prefetch=1, grid=(nb, n_pages),
            in_specs=[per_b(1, ATTN_W), per_b(1, ATTN_W),
                      pl.BlockSpec((1, PAGE_SIZE, ATTN_W), lambda b, g, pt: (pt[b * n_pages + g], 0, 0))],
            out_specs=[per_b(n_pages, ATTN_HEADS, V7X_LANES), per_b(MOBA_TOPK, ATTN_HEADS, V7X_LANES),
                       per_b(ATTN_HEADS, V7X_LANES)],
            scratch_shapes=[pltpu.VMEM((ATTN_HEADS, V7X_LANES), F32)]),
        out_shape=[jax.ShapeDtypeStruct((nb, n_pages, ATTN_HEADS, V7X_LANES), F32),
                   jax.ShapeDtypeStruct((nb, MOBA_TOPK, ATTN_HEADS, V7X_LANES), jnp.int32),
                   jax.ShapeDtypeStruct((nb, ATTN_HEADS, V7X_LANES), F32)],
        compiler_params=_cparams("arbitrary", "arbitrary"), name="moba_paged_scores",
    )(pt, row3(qs.astype(F32)), row3(k_new), cache_k)
    slot_ids = jnp.transpose(ids[:, :, :, 0], (0, 2, 1)).reshape(-1)
    n_slots = ATTN_HEADS * MOBA_TOPK

    def v_page(half):
        def index(b, s, pt, sid):
            return (pt[b * n_pages + jnp.maximum(sid[b * n_slots + s], 0) * PAGES_PER_BLOCK + half], 0, 0)
        return pl.BlockSpec((1, PAGE_SIZE, ATTN_W), index)

    return pl.pallas_call(
        _moba_paged_pv_body,
        grid_spec=pltpu.PrefetchScalarGridSpec(
            num_scalar_prefetch=2, grid=(nb, n_slots),
            in_specs=[per_b(n_pages, ATTN_HEADS, V7X_LANES), per_b(ATTN_HEADS, V7X_LANES), per_b(1, ATTN_W),
                      v_page(0), v_page(1)],
            out_specs=per_b(1, ATTN_W),
            scratch_shapes=[pltpu.VMEM((ATTN_HEADS, ATTN_W), F32)]),
        out_shape=jax.ShapeDtypeStruct((nb, 1, ATTN_W), MXU_DTYPE),
        compiler_params=_cparams("arbitrary", "arbitrary"), name="moba_paged_pv",
    )(pt, slot_ids, probs, p_new, row3(v_new), cache_v, cache_v).reshape(nb, ATTN_W)


def _gmlp_in_body(seq_mode, tm, *refs):
    if seq_mode:
        x_ref, g_ref, w_ref, lg_ref, lb_ref, ws_ref, bias_ref, act_ref = refs
    else:
        x_ref, g_ref, w_ref, lg_ref, lb_ref, wrow_ref, brow_ref, act_ref, v_ref = refs
    xn = _rms(x_ref[...], g_ref[...]).astype(MXU_DTYPE)
    u = jax.nn.gelu(_mm(xn, w_ref[:, :GMLP_W]))
    v = jax.nn.gelu(_mm(xn, w_ref[:, GMLP_W:]))
    vc = v - jnp.mean(v, axis=-1, keepdims=True)
    v = vc * lax.rsqrt(jnp.mean(vc * vc, axis=-1, keepdims=True) + EPS) * lg_ref[...] + lb_ref[...]
    if not seq_mode:
        v_ref[...] = v
        act_ref[...] = (u * (v * wrow_ref[...] + brow_ref[...])).astype(act_ref.dtype)
        return
    dg = GMLP_W // GMLP_GROUPS
    r = lax.broadcasted_iota(jnp.int32, (GMLP_CHUNK, GMLP_CHUNK), 0)
    c = lax.broadcasted_iota(jnp.int32, (GMLP_CHUNK, GMLP_CHUNK), 1)
    vb = v.astype(MXU_DTYPE)
    for g in range(GMLP_GROUPS):
        ws = jnp.where(c <= r, ws_ref[g], 0.0).astype(MXU_DTYPE)
        cols = slice(g * dg, (g + 1) * dg)
        for ch in range(tm // GMLP_CHUNK):
            rows = slice(ch * GMLP_CHUNK, (ch + 1) * GMLP_CHUNK)
            s = _mm(ws, vb[rows, cols]) + bias_ref[:, cols]
            act_ref[rows, cols] = (u[rows, cols] * s).astype(act_ref.dtype)


def _gmlp_in(x, gain, w_in, ln_g, ln_b, *, seq_mode, w_s=None, bias=None, wrow=None, brow=None):
    t = x.shape[0]
    tm = min(ROW_TILE, t)
    assert t % tm == 0 and (not seq_mode or tm % GMLP_CHUNK == 0)
    vec = _full((1, GMLP_W))
    in_specs = [_rows(tm, D_MODEL), vec, _full(w_in.shape), vec, vec]
    act = jax.ShapeDtypeStruct((t, GMLP_W), MXU_DTYPE)
    if seq_mode:
        in_specs += [_full(w_s.shape), _full(bias.shape)]
        args = [x, gain, w_in, ln_g, ln_b, w_s, bias]
        out_shape, out_specs = act, _rows(tm, GMLP_W)
    else:
        in_specs += [vec, vec]
        args = [x, gain, w_in, ln_g, ln_b, wrow, brow]
        out_shape = [act, jax.ShapeDtypeStruct((t, GMLP_W), F32)]
        out_specs = [_rows(tm, GMLP_W)] * 2
    return pl.pallas_call(
        functools.partial(_gmlp_in_body, seq_mode, tm), grid=(t // tm,), in_specs=in_specs,
        out_specs=out_specs, out_shape=out_shape, compiler_params=_cparams("arbitrary"),
        name="gmlp_in_seq" if seq_mode else "gmlp_in_rows")(*args)


def _mem_kv_body(m_ref, g_ref, wk_ref, wv_ref, gk_ref, k_ref, v_ref, kb_ref, vb_ref):
    mm = _rms(m_ref[...], g_ref[...]).astype(MXU_DTYPE)
    k = _mm(mm, wk_ref[...])
    v = _mm(mm, wv_ref[...])
    k = jnp.concatenate(
        [_rms(k[:, h * MEM_HEAD_DIM:(h + 1) * MEM_HEAD_DIM], gk_ref[...]) for h in range(MEM_HEADS)], axis=1)
    k_ref[...] = k
    v_ref[...] = v
    kb_ref[...] = k.astype(kb_ref.dtype)
    vb_ref[...] = v.astype(vb_ref.dtype)


def _mem_kv(mem, gain, w_k, w_v, g_k):
    m = mem.shape[0]
    f32o = jax.ShapeDtypeStruct((m, D_MODEL), F32)
    b16o = jax.ShapeDtypeStruct((m, D_MODEL), MXU_DTYPE)
    blk = _full((m, D_MODEL))
    return pl.pallas_call(
        _mem_kv_body, grid=(1,),
        in_specs=[blk, _full((1, D_MODEL)), _full(w_k.shape), _full(w_v.shape), _full((1, MEM_HEAD_DIM))],
        out_specs=[blk] * 4, out_shape=[f32o, f32o, b16o, b16o],
        compiler_params=_cparams("arbitrary"), name="mem_kv")(mem, gain, w_k, w_v, g_k)


def _post_mix_body(n_act, shared_mem, *refs):
    h_ref = refs[0]
    act_refs = refs[1:1 + n_act]
    w_refs = refs[1 + n_act:1 + 2 * n_act]
    rest = refs[1 + 2 * n_act:]
    if shared_mem:
        g_ref, wq_ref, gq_ref, mk_ref, mv_ref, wo_ref, o_ref = rest
    else:
        g_ref, wq_ref, gq_ref, h1_ref, q_ref = rest
    h1 = h_ref[...]
    for a_ref, w_ref in zip(act_refs, w_refs):
        h1 = h1 + _mm(a_ref[...], w_ref[...])
    qc = _mm(_rms(h1, g_ref[...]).astype(MXU_DTYPE), wq_ref[...])
    heads = []
    for hd in range(MEM_HEADS):
        cols = slice(hd * MEM_HEAD_DIM, (hd + 1) * MEM_HEAD_DIM)
        qn = _rms(qc[:, cols], gq_ref[...])
        if not shared_mem:
            heads.append(qn)
            continue
        s = lax.dot_general(qn.astype(MXU_DTYPE), mk_ref[:, cols], NT_DIMS,
                            preferred_element_type=F32) * (MEM_HEAD_DIM ** -0.5)
        p = jnp.exp(s - jnp.max(s, axis=1, keepdims=True))
        o = _mm(p.astype(MXU_DTYPE), mv_ref[:, cols]) / jnp.sum(p, axis=1, keepdims=True)
        heads.append(o.astype(MXU_DTYPE))
    cat = jnp.concatenate(heads, axis=1)
    if shared_mem:
        o_ref[...] = h1 + _mm(cat, wo_ref[...])
    else:
        h1_ref[...] = h1
        q_ref[...] = cat


def _post_mix(h, acts, w_outs, gain, w_q, g_q, mem=None, w_o=None):
    t = h.shape[0]
    tm = min(ROW_TILE, t)
    assert t % tm == 0
    shared = mem is not None
    in_specs = [_rows(tm, D_MODEL)] + [_rows(tm, a.shape[1]) for a in acts] + [_full(w.shape) for w in w_outs]
    in_specs += [_full((1, D_MODEL)), _full(w_q.shape), _full((1, MEM_HEAD_DIM))]
    args = [h, *acts, *w_outs, gain, w_q, g_q]
    f32o = jax.ShapeDtypeStruct((t, D_MODEL), F32)
    if shared:
        in_specs += [_full(mem[0].shape), _full(mem[1].shape), _full(w_o.shape)]
        args += [mem[0], mem[1], w_o]
        out_shape, out_specs = f32o, _rows(tm, D_MODEL)
    else:
        out_shape, out_specs = [f32o, f32o], [_rows(tm, D_MODEL)] * 2
    return pl.pallas_call(
        functools.partial(_post_mix_body, len(acts), shared), grid=(t // tm,), in_specs=in_specs,
        out_specs=out_specs, out_shape=out_shape, compiler_params=_cparams("arbitrary"),
        name="post_mix_shared" if shared else "post_mix_rows")(*args)


def _mem_attend_rows_body(q_ref, k_ref, v_ref, o_ref):
    prod = k_ref[0] * q_ref[0]
    outs = []
    for hd in range(MEM_HEADS):
        cols = slice(hd * MEM_HEAD_DIM, (hd + 1) * MEM_HEAD_DIM)
        s = jnp.sum(prod[:, cols], axis=1, keepdims=True) * (MEM_HEAD_DIM ** -0.5)
        p = jnp.exp(s - jnp.max(s, axis=0, keepdims=True))
        o = jnp.sum(p * v_ref[0][:, cols], axis=0, keepdims=True) / jnp.sum(p, axis=0, keepdims=True)
        outs.append(o)
    o_ref[0] = jnp.concatenate(outs, axis=1).astype(o_ref.dtype)


def _mem_attend_rows(qn, mem_k, mem_v, first):
    nb, m = qn.shape[0], mem_k.shape[1]
    row = pl.BlockSpec((1, 1, D_MODEL), lambda b: (b, 0, 0))
    memb = pl.BlockSpec((1, m, D_MODEL), lambda b: (first + b, 0, 0))
    return pl.pallas_call(
        _mem_attend_rows_body, grid=(nb,), in_specs=[row, memb, memb], out_specs=row,
        out_shape=jax.ShapeDtypeStruct((nb, 1, D_MODEL), MXU_DTYPE),
        compiler_params=_cparams("arbitrary"), name="mem_attend_rows",
    )(qn.reshape(nb, 1, D_MODEL), mem_k, mem_v).reshape(nb, D_MODEL)


def _ffn_body(pre_proj, *refs):
    if pre_proj:
        h_ref, o_in_ref, wo_ref, g_ref, w1_ref, w2_ref, out_ref, h_scr, xn_scr, acc = refs
    else:
        h_ref, g_ref, w1_ref, w2_ref, out_ref, h_scr, xn_scr, acc = refs
    f = pl.program_id(1)

    @pl.when(f == 0)
    def _():
        h = h_ref[...]
        if pre_proj:
            h = h + _mm(o_in_ref[...], wo_ref[...])
        h_scr[...] = h
        xn_scr[...] = _rms(h, g_ref[...]).astype(xn_scr.dtype)
        acc[...] = jnp.zeros_like(acc)

    a = jnp.maximum(_mm(xn_scr[...], w1_ref[...]), 0.0)
    acc[...] += _mm((a * a).astype(MXU_DTYPE), w2_ref[...])

    @pl.when(f == pl.num_programs(1) - 1)
    def _():
        out_ref[...] = h_scr[...] + acc[...]


def _ffn(h, gain, w1, w2, pre=None):
    t = h.shape[0]
    tm = min(FFN_ROW_TILE, t)
    tf = FFN_COL_TILE
    assert t % tm == 0 and FFN_W % tf == 0
    rows = pl.BlockSpec((tm, D_MODEL), lambda i, f: (i, 0))
    in_specs, args = [rows], [h]
    if pre is not None:
        in_specs += [rows, pl.BlockSpec(pre[1].shape, lambda i, f: (0, 0))]
        args += list(pre)
    in_specs += [pl.BlockSpec((1, D_MODEL), lambda i, f: (0, 0)),
                 pl.BlockSpec((D_MODEL, tf), lambda i, f: (0, f)), pl.BlockSpec((tf, D_MODEL), lambda i, f: (f, 0))]
    args += [gain, w1, w2]
    return pl.pallas_call(
        functools.partial(_ffn_body, pre is not None), grid=(t // tm, FFN_W // tf), in_specs=in_specs,
        out_specs=rows, out_shape=jax.ShapeDtypeStruct((t, D_MODEL), F32),
        scratch_shapes=[pltpu.VMEM((tm, D_MODEL), F32), pltpu.VMEM((tm, D_MODEL), MXU_DTYPE),
                        pltpu.VMEM((tm, D_MODEL), F32)],
        compiler_params=_cparams("arbitrary", "arbitrary"), name="ffn")(*args)


def kernel(x_prompt, x_sample, mem_prompt, cache_attn_k, cache_attn_v, state_conv, cache_mem_k, cache_mem_v,
           page_table, norm_mix, norm_cross, norm_mem, norm_ffn, mix_w_in, attn_g_q, attn_g_k, conv_w, mix_w_out,
           gmlp_w_in, gmlp_ln_g, gmlp_ln_b, gmlp_w_s, gmlp_b_s, gmlp_w_out, cross_w_q, cross_w_k, cross_w_v,
           cross_w_o, cross_g_q, cross_g_k, ffn_w1, ffn_w2):
    b_p, t_p, _ = x_prompt.shape
    b_s, t_s, _ = x_sample.shape
    assert b_p == 1 and t_s == 1
    depth = norm_mix.shape[0]
    n_pages = page_table.shape[1]
    past_len = n_pages * PAGE_SIZE
    bf = lambda a: a.astype(MXU_DTYPE)
    vec = lambda a: a.reshape(1, -1)

    half = ROT_DIM // 2
    inv = jnp.power(jnp.float32(ROPE_THETA), -jnp.arange(half, dtype=jnp.float32) * (2.0 / ROT_DIM))
    lane = jnp.arange(V7X_LANES) % HEAD_DIM
    inv_lanes = jnp.where(lane < ROT_DIM, inv[lane % half], 0.0).reshape(1, V7X_LANES).astype(F32)
    hmat = bf(jnp.kron(jnp.eye(ATTN_HEADS, dtype=F32), jnp.ones((HEAD_DIM, HEAD_DIM), F32)))

    hp = x_prompt.reshape(t_p, D_MODEL)
    hs = x_sample.reshape(b_s, D_MODEL)
    ak_p, av_p, ak_s, av_s, cs_p, cs_s, gv_s, mk_p, mv_p = [], [], [], [], [], [], [], [], []
    for layer in range(depth):
        li = layer // 2
        gmix = vec(norm_mix[layer])
        if layer % 2 == 0:
            w_in = bf(mix_w_in[li])
            gq = vec(jnp.tile(attn_g_q[li], ATTN_HEADS))
            gk = vec(jnp.tile(attn_g_k[li], ATTN_HEADS))
            w_out = bf(mix_w_out[li])
            w_outs = [w_out[:ATTN_W], w_out[ATTN_W:]]
            qs, k, v, kb, vb, cmix, kmean, tail = _mix_in(
                hp, gmix, w_in, gq, gk, hmat, inv_lanes, conv_w[li], seq_mode=True, pos_base=0, pos_stride=1)
            kmean = kmean.reshape(-1, ATTN_W)
            kmean_pad = bf(jnp.pad(kmean, ((0, V7X_LANES - kmean.shape[0]), (0, 0))))
            acts_p = [_moba_seq(qs, kb, vb, kmean_pad), cmix]
            ak_p.append(k.reshape(b_p, t_p, ATTN_HEADS, HEAD_DIM))
            av_p.append(v.reshape(b_p, t_p, ATTN_HEADS, HEAD_DIM))
            cs_p.append(tail[8 - (CONV_WIDTH - 1):].reshape(b_p, CONV_WIDTH - 1, CONV_W))

            qs_s, k_s, v_s, cmix_s, u_s = _mix_in(
                hs, gmix, w_in, gq, gk, hmat, inv_lanes, conv_w[li], seq_mode=False, pos_base=past_len,
                pos_stride=0, prev=(state_conv[li, :, 0], state_conv[li, :, 1]))
            pool = cache_attn_k.shape[1]
            attn_s = _moba_paged(qs_s, k_s, v_s, cache_attn_k.reshape(-1, PAGE_SIZE, ATTN_W),
                                 cache_attn_v.reshape(-1, PAGE_SIZE, ATTN_W), page_table + li * pool)
            acts_s = [attn_s, cmix_s]
            ak_s.append(k_s.reshape(b_s, t_s, ATTN_HEADS, HEAD_DIM))
            av_s.append(v_s.reshape(b_s, t_s, ATTN_HEADS, HEAD_DIM))
            cs_s.append(jnp.stack([state_conv[li, :, 1], u_s], axis=1))
        else:
            w_in = bf(gmlp_w_in[li])
            w_outs = [bf(gmlp_w_out[li])]
            lg, lb = vec(gmlp_ln_g[li]), vec(gmlp_ln_b[li])
            dg = GMLP_W // GMLP_GROUPS
            bias = jnp.repeat(gmlp_b_s[li].T, dg, axis=1)
            acts_p = [_gmlp_in(hp, gmix, w_in, lg, lb, seq_mode=True, w_s=gmlp_w_s[li], bias=bias)]
            wrow = vec(jnp.repeat(gmlp_w_s[li][:, 0, 0], dg))
            brow = vec(jnp.repeat(gmlp_b_s[li][:, 0], dg))
            act_s, gv = _gmlp_in(hs, gmix, w_in, lg, lb, seq_mode=False, wrow=wrow, brow=brow)
            acts_s = [act_s]
            gv_s.append(gv.reshape(b_s, t_s, GMLP_W))

        gcross, w_q, g_q = vec(norm_cross[layer]), bf(cross_w_q[layer]), vec(cross_g_q[layer])
        w_o = bf(cross_w_o[layer])
        mk, mv, mkb, mvb = _mem_kv(mem_prompt.reshape(-1, D_MODEL), vec(norm_mem[layer]), bf(cross_w_k[layer]),
                                   bf(cross_w_v[layer]), vec(cross_g_k[layer]))
        mk_p.append(mk.reshape(b_p, -1, MEM_HEADS, MEM_HEAD_DIM))
        mv_p.append(mv.reshape(b_p, -1, MEM_HEADS, MEM_HEAD_DIM))
        gffn, w1, w2 = vec(norm_ffn[layer]), bf(ffn_w1[layer]), bf(ffn_w2[layer])

        hp = _post_mix(hp, acts_p, w_outs, gcross, w_q, g_q, mem=(mkb, mvb), w_o=w_o)
        hp = _ffn(hp, gffn, w1, w2)

        h1_s, qn_s = _post_mix(hs, acts_s, w_outs, gcross, w_q, g_q)
        m_len = cache_mem_k.shape[2]
        o_s = _mem_attend_rows(qn_s, cache_mem_k.reshape(-1, m_len, D_MODEL),
                               cache_mem_v.reshape(-1, m_len, D_MODEL), layer * b_s)
        hs = _ffn(h1_s, gffn, w1, w2, pre=(o_s, w_o))

    return (hp.reshape(b_p, t_p, D_MODEL), hs.reshape(b_s, t_s, D_MODEL), jnp.stack(ak_p), jnp.stack(av_p),
            jnp.stack(ak_s), jnp.stack(av_s), jnp.stack(cs_p), jnp.stack(cs_s), jnp.stack(gv_s),
            jnp.stack(mk_p), jnp.stack(mv_p))
```

```python
import functools
import math

import jax
import jax.numpy as jnp
from jax import lax
from jax.experimental import pallas as pl
from jax.experimental.pallas import tpu as pltpu

D_MODEL = 1024
ATTN_HEADS = 8
HEAD_DIM = 64
ATTN_W = ATTN_HEADS * HEAD_DIM
ROT_DIM = HEAD_DIM // 4
ROPE_THETA = 500000.0
MOBA_BLOCK = 256
MOBA_TOPK = 3
PAGE_SIZE = 128
CONV_W = D_MODEL - ATTN_W
CONV_WIDTH = 3
GMLP_W = D_MODEL
GMLP_GROUPS = 4
GMLP_CHUNK = 128
MEM_HEADS = 4
MEM_HEAD_DIM = D_MODEL // MEM_HEADS
FFN_W = 4 * D_MODEL
EPS = 1e-6

MXU_DTYPE = jnp.bfloat16
V7X_LANES = 128
V7X_VMEM_BYTES = 64 * 1024 * 1024
VMEM_LIMIT = V7X_VMEM_BYTES * 7 // 8
NEG = -1e30
F32 = jnp.float32

ROW_TILE = 512
FFN_ROW_TILE = 1024
FFN_COL_TILE = 1024
PAGES_PER_BLOCK = MOBA_BLOCK // PAGE_SIZE
SCORE_PAGES_PER_STEP = 16
NT_DIMS = (((1,), (1,)), ((), ()))
LOG2_SCORE_SCALE = HEAD_DIM ** -0.5 * math.log2(math.e)


def _cparams(*semantics):
    return pltpu.CompilerParams(dimension_semantics=semantics, vmem_limit_bytes=VMEM_LIMIT)


def _full(shape):
    n = len(shape)
    return pl.BlockSpec(shape, lambda *_: (0,) * n)


def _rows(tm, width):
    return pl.BlockSpec((tm, width), lambda i: (i, 0))


def _rms(x, g):
    return x * lax.rsqrt(jnp.mean(x * x, axis=-1, keepdims=True) + EPS) * g


def _mm(a, b):
    return jnp.dot(a, b, preferred_element_type=F32)


def _top_ids(g, lanef):
    ids = []
    for _ in range(MOBA_TOPK):
        mx = jnp.max(g, axis=1, keepdims=True)
        idx = jnp.min(jnp.where(g == mx, lanef, float(V7X_LANES)), axis=1, keepdims=True)
        ids.append(jnp.where(mx > 0.5 * NEG, idx, -1.0))
        g = jnp.where(lanef == idx, NEG, g)
    return ids


def _head_rms_rows(t, g, hm):
    t2 = t * t
    hi = t2.astype(MXU_DTYPE)
    lo = (t2 - hi.astype(F32)).astype(MXU_DTYPE)
    ss = _mm(hi, hm) + _mm(lo, hm)
    return t * lax.rsqrt(ss * (1.0 / HEAD_DIM) + EPS) * g


def _rope_rows(t, pos, inv_lanes):
    ang = pos * inv_lanes
    cs, sn = jnp.cos(ang), jnp.sin(ang)
    lane = lax.broadcasted_iota(jnp.int32, ang.shape, 1) & (HEAD_DIM - 1)
    half = ROT_DIM // 2
    reps = ATTN_W // V7X_LANES
    c_t = jnp.concatenate([jnp.where(lane < ROT_DIM, cs, 1.0)] * reps, axis=1)
    s_up = jnp.concatenate([jnp.where((lane >= half) & (lane < ROT_DIM), sn, 0.0)] * reps, axis=1)
    s_dn = jnp.concatenate([jnp.where(lane < half, -sn, 0.0)] * reps, axis=1)
    return t * c_t + pltpu.roll(t, half, 1) * s_up + pltpu.roll(t, ATTN_W - half, 1) * s_dn


def _conv_taps(u2, u1, u, cw):
    return u2 * cw[0:1, :] + u1 * cw[1:2, :] + u * cw[2:3, :]


def _mix_in_seq_body(tm, x_ref, g_ref, w_ref, wt_ref, gqc_ref, gk_ref, hm_ref, inv_ref, inv8_ref, cw_ref,
                     qt_ref, kt_ref, kb_ref, vt_ref, vtb_ref, cm_ref, kmean_ref, tail_ref, ubuf):
    i = pl.program_id(0)
    xn = _rms(x_ref[...], g_ref[...]).astype(MXU_DTYPE)
    zt = lax.dot_general(wt_ref[...], xn, NT_DIMS, preferred_element_type=F32)
    post = (i * tm + lax.broadcasted_iota(jnp.int32, (1, tm), 1)).astype(F32)
    ang = inv8_ref[...] * post
    cs, sn = jnp.cos(ang), jnp.sin(ang)
    half = ROT_DIM // 2
    pieces = []
    for h in range(ATTN_HEADS):
        t = zt[h * HEAD_DIM:(h + 1) * HEAD_DIM]
        t = t * lax.rsqrt(jnp.mean(t * t, axis=0, keepdims=True) + EPS) * gqc_ref[h * HEAD_DIM:(h + 1) * HEAD_DIM]
        x1, x2 = t[0:half], t[half:ROT_DIM]
        pieces += [x1 * cs - x2 * sn, x2 * cs + x1 * sn, t[ROT_DIM:]]
    qt_ref[...] = (jnp.concatenate(pieces, axis=0) * LOG2_SCORE_SCALE).astype(qt_ref.dtype)
    vt = zt[ATTN_W:]
    vt_ref[...] = vt
    for b in range(tm // MOBA_BLOCK):
        vtb_ref[b] = vt[:, b * MOBA_BLOCK:(b + 1) * MOBA_BLOCK].astype(vtb_ref.dtype)

    def proj(c):
        return _mm(xn, w_ref[:, c * ATTN_W:(c + 1) * ATTN_W])

    row = lax.broadcasted_iota(jnp.int32, (tm, V7X_LANES), 0)
    k = _rope_rows(_head_rms_rows(proj(0), gk_ref[...], hm_ref[...]), (i * tm + row).astype(F32), inv_ref[...])
    kt_ref[...] = k.T
    kb_ref[...] = k.astype(kb_ref.dtype)
    for b in range(tm // MOBA_BLOCK):
        kmean_ref[0, b:b + 1, :] = jnp.mean(k[b * MOBA_BLOCK:(b + 1) * MOBA_BLOCK], axis=0, keepdims=True)

    bg = proj(1)
    u = proj(2) * proj(3)

    @pl.when(i == 0)
    def _():
        ubuf[0:8, :] = jnp.zeros((8, CONV_W), F32)

    ubuf[8:8 + tm, :] = u
    conv = _conv_taps(ubuf[6:6 + tm, :], ubuf[7:7 + tm, :], u, cw_ref[...])
    ubuf[0:8, :] = u[tm - 8:tm, :]
    tail_ref[...] = u[tm - 8:tm, :]
    cm_ref[...] = (bg * conv).astype(cm_ref.dtype)


def _mix_in_seq(x, gain, w_rows, w_t, gq_col, gk, hmat, inv_lanes, inv8, conv_w):
    t = x.shape[0]
    tm = min(ROW_TILE, t)
    assert t % tm == 0 and tm % MOBA_BLOCK == 0
    n, nb = t // tm, tm // MOBA_BLOCK
    f32t = jax.ShapeDtypeStruct((ATTN_W, t), F32)
    b16o = jax.ShapeDtypeStruct((t, ATTN_W), MXU_DTYPE)
    cols = pl.BlockSpec((ATTN_W, tm), lambda i: (0, i))
    return pl.pallas_call(
        functools.partial(_mix_in_seq_body, tm), grid=(n,),
        in_specs=[_rows(tm, D_MODEL), _full((1, D_MODEL)), _full(w_rows.shape), _full(w_t.shape),
                  _full((ATTN_W, 1)), _full((1, ATTN_W)), _full(hmat.shape), _full((1, V7X_LANES)),
                  _full((ROT_DIM // 2, 1)), _full((CONV_WIDTH, CONV_W))],
        out_specs=[cols, cols, _rows(tm, ATTN_W), cols,
                   pl.BlockSpec((nb, ATTN_W, MOBA_BLOCK), lambda i: (i, 0, 0)), _rows(tm, CONV_W),
                   pl.BlockSpec((1, nb, ATTN_W), lambda i: (i, 0, 0)), _full((8, CONV_W))],
        out_shape=[jax.ShapeDtypeStruct((ATTN_W, t), MXU_DTYPE), f32t, b16o, f32t,
                   jax.ShapeDtypeStruct((t // MOBA_BLOCK, ATTN_W, MOBA_BLOCK), MXU_DTYPE), b16o,
                   jax.ShapeDtypeStruct((n, nb, ATTN_W), F32), jax.ShapeDtypeStruct((8, CONV_W), F32)],
        scratch_shapes=[pltpu.VMEM((tm + 8, CONV_W), F32)],
        compiler_params=_cparams("arbitrary"), name="mix_in_seq",
    )(x, gain, w_rows, w_t, gq_col, gk, hmat, inv_lanes, inv8, conv_w)


def _mix_in_rows_body(pos, x_ref, g_ref, w_ref, gq_ref, gk_ref, hm_ref, inv_ref, cw_ref, p2_ref, p1_ref,
                      qs_ref, k_ref, v_ref, cm_ref, u_ref):
    xn = _rms(x_ref[...], g_ref[...]).astype(MXU_DTYPE)

    def proj(c):
        return _mm(xn, w_ref[:, c * ATTN_W:(c + 1) * ATTN_W])

    posf = jnp.full((x_ref.shape[0], V7X_LANES), pos, F32)
    q = _rope_rows(_head_rms_rows(proj(0), gq_ref[...], hm_ref[...]), posf, inv_ref[...])
    k = _rope_rows(_head_rms_rows(proj(1), gk_ref[...], hm_ref[...]), posf, inv_ref[...])
    qs_ref[...] = q * (HEAD_DIM ** -0.5)
    k_ref[...] = k
    v_ref[...] = proj(2)
    bg = proj(3)
    u = proj(4) * proj(5)
    u_ref[...] = u
    cm_ref[...] = (bg * _conv_taps(p2_ref[...], p1_ref[...], u, cw_ref[...])).astype(cm_ref.dtype)


def _mix_in_rows(x, gain, w_in, gq, gk, hmat, inv_lanes, conv_w, prev2, prev1, *, pos):
    t = x.shape[0]
    f32o = jax.ShapeDtypeStruct((t, ATTN_W), F32)
    blk = _rows(t, ATTN_W)
    return pl.pallas_call(
        functools.partial(_mix_in_rows_body, pos), grid=(1,),
        in_specs=[_rows(t, D_MODEL), _full((1, D_MODEL)), _full(w_in.shape), _full((1, ATTN_W)), _full((1, ATTN_W)),
                  _full(hmat.shape), _full((1, V7X_LANES)), _full((CONV_WIDTH, CONV_W)), blk, blk],
        out_specs=[blk] * 5,
        out_shape=[f32o, f32o, f32o, jax.ShapeDtypeStruct((t, ATTN_W), MXU_DTYPE), f32o],
        compiler_params=_cparams("arbitrary"), name="mix_in_rows",
    )(x, gain, w_in, gq, gk, hmat, inv_lanes, conv_w, prev2, prev1)


def _moba_seq_body(qt_ref, k_ref, vt_ref, km_ref, o_ref, m_scr, acc_scr, id_scr, s_even, s_odd):
    i = pl.program_id(1)
    blk = MOBA_BLOCK
    last = vt_ref.shape[0] - 1
    qt = qt_ref[...]
    frow = lax.broadcasted_iota(jnp.int32, (V7X_LANES, blk), 0)
    qh = [jnp.where((frow < HEAD_DIM) == (h == 0), qt, jnp.zeros_like(qt)) for h in range(2)]
    brow = frow.astype(F32)
    krow = lax.broadcasted_iota(jnp.int32, (blk, blk), 0)
    qcol = lax.broadcasted_iota(jnp.int32, (blk, blk), 1)

    def keys(j):
        return k_ref[pl.ds(pl.multiple_of(j * blk, blk), blk), :]

    def values(vt, h):
        r = lax.broadcasted_iota(jnp.int32, vt.shape, 0)
        return jnp.where((r < HEAD_DIM) == (h == 0), vt, jnp.ones_like(vt))

    def raw_scores(dst, step):
        for d in range(2):
            kd = keys(jnp.minimum(2 * step + d, last))
            for h in range(2):
                dst[h, d] = _mm(kd, qh[h])

    def consume(src, step):
        j0 = 2 * step
        vs = jnp.concatenate([vt_ref[jnp.minimum(j0 + d, last)] for d in range(2)], axis=1)
        for h in range(2):
            ss, picked = [], []
            for d in range(2):
                jf = (j0 + d).astype(F32)
                picked.append((id_scr[h, 0:1, :] == jf) | (id_scr[h, 1:2, :] == jf) | (id_scr[h, 2:3, :] == jf))
                ss.append(src[h, d])
            m_prev = m_scr[h][0:1]
            m_new = m_prev
            for d in range(2):
                m_new = jnp.maximum(m_new, jnp.where(picked[d], jnp.max(ss[d], axis=0, keepdims=True), NEG))
            p = jnp.concatenate([jnp.exp2(ss[d] - jnp.where(picked[d], m_new, -NEG)) for d in range(2)],
                                axis=0).astype(MXU_DTYPE)
            acc_scr[h] = jnp.exp2(m_prev - m_new) * acc_scr[h] + _mm(values(vs, h), p)
            m_scr[h] = jnp.broadcast_to(m_new, (8, blk))

    k_own = keys(i)
    v_own = vt_ref[i]
    for h in range(2):
        gate = jnp.where(frow < i, _mm(km_ref[...], qh[h]), NEG)
        for r in range(MOBA_TOPK):
            mx = jnp.max(gate, axis=0, keepdims=True)
            idx = jnp.min(jnp.where(gate == mx, brow, float(V7X_LANES)), axis=0, keepdims=True)
            id_scr[h, r:r + 1, :] = jnp.where(mx > 0.5 * NEG, idx, -1.0)
            gate = jnp.where(brow == idx, NEG, gate)
        s = jnp.where(krow <= qcol, _mm(k_own, qh[h]), NEG)
        m = jnp.max(s, axis=0, keepdims=True)
        m_scr[h] = jnp.broadcast_to(m, (8, blk))
        acc_scr[h] = _mm(values(v_own, h), jnp.exp2(s - m).astype(MXU_DTYPE))

    raw_scores(s_even, 0)

    def two_steps(t, carry):
        raw_scores(s_odd, 2 * t + 1)
        consume(s_even, 2 * t)
        raw_scores(s_even, 2 * t + 2)
        consume(s_odd, 2 * t + 1)
        return carry

    lax.fori_loop(0, (i + 3) // 4, two_steps, 0)
    a, b = acc_scr[0], acc_scr[1]
    ot = jnp.concatenate([a[:HEAD_DIM] / a[HEAD_DIM:HEAD_DIM + 1], b[HEAD_DIM:] / b[0:1]], axis=0)
    o_ref[...] = ot.T.astype(o_ref.dtype)


def _moba_seq(qt, kb, vtb, kmean_pad):
    t = kb.shape[0]
    nblk = t // MOBA_BLOCK
    assert t % MOBA_BLOCK == 0 and nblk <= V7X_LANES
    pairs = ATTN_W // V7X_LANES
    return pl.pallas_call(
        _moba_seq_body,
        grid=(pairs, nblk),
        in_specs=[pl.BlockSpec((V7X_LANES, MOBA_BLOCK), lambda p, i: (p, i)),
                  pl.BlockSpec((t, V7X_LANES), lambda p, i: (0, p)),
                  pl.BlockSpec((nblk, V7X_LANES, MOBA_BLOCK), lambda p, i: (0, p, 0)),
                  pl.BlockSpec((V7X_LANES, V7X_LANES), lambda p, i: (0, p))],
        out_specs=pl.BlockSpec((MOBA_BLOCK, V7X_LANES), lambda p, i: (i, p)),
        out_shape=jax.ShapeDtypeStruct((t, ATTN_W), MXU_DTYPE),
        scratch_shapes=[pltpu.VMEM((2, 8, MOBA_BLOCK), F32), pltpu.VMEM((2, V7X_LANES, MOBA_BLOCK), F32),
                        pltpu.VMEM((2, 8, MOBA_BLOCK), F32)] + [pltpu.VMEM((2, 2, MOBA_BLOCK, MOBA_BLOCK), F32)] * 2,
        compiler_params=_cparams("arbitrary", "arbitrary"), name="moba_seq")(qt, kb, vtb, kmean_pad)


def _head_rows(q_row):
    sub = lax.broadcasted_iota(jnp.int32, (ATTN_HEADS, ATTN_W), 0)
    lane = lax.broadcasted_iota(jnp.int32, (ATTN_HEADS, ATTN_W), 1)
    return jnp.where(lane // HEAD_DIM == sub, jnp.broadcast_to(q_row, (ATTN_HEADS, ATTN_W)), 0.0)


def _moba_paged_scores_body(n_pages, pt_ref, q_ref, kn_ref, k_hbm, p_ref, ids_ref, pn_ref, kbuf, sem, gsum):
    per = SCORE_PAGES_PER_STEP
    groups = n_pages // per
    t = pl.program_id(0)
    grp = t % groups
    slot = t % 2

    def page_copy(step, buf, g):
        return pltpu.make_async_copy(k_hbm.at[pt_ref[step * per + g]], kbuf.at[buf, g], sem.at[buf])

    @pl.when(t == 0)
    def _():
        for g in range(per):
            page_copy(t, slot, g).start()

    @pl.when(t + 1 < pl.num_programs(0))
    def _():
        for g in range(per):
            page_copy(t + 1, 1 - slot, g).start()

    for g in range(per):
        page_copy(t, slot, g).wait()

    qd = _head_rows(q_ref[0])
    qb = qd.astype(MXU_DTYPE)
    lane = lax.broadcasted_iota(jnp.int32, (ATTN_HEADS, V7X_LANES), 1)

    @pl.when(grp == 0)
    def _():
        gsum[...] = jnp.zeros_like(gsum)

    for g in range(per):
        pg = grp * per + g
        s = _mm(qb, kbuf[slot, g].astype(MXU_DTYPE))
        p_ref[0, pg] = s
        gsum[...] += jnp.where(lane == pg // PAGES_PER_BLOCK, jnp.sum(s, axis=1, keepdims=True), 0.0)

    @pl.when(grp == groups - 1)
    def _():
        n_blocks = n_pages // PAGES_PER_BLOCK
        lanef = lane.astype(F32)
        gate = jnp.where(lane < n_blocks, gsum[...] * (1.0 / MOBA_BLOCK), NEG)
        ids = _top_ids(gate, lanef)
        for r in range(MOBA_TOPK):
            ids_ref[0, r] = jnp.broadcast_to(ids[r], (ATTN_HEADS, V7X_LANES)).astype(jnp.int32)
        s_new = jnp.sum(qd * kn_ref[0], axis=1, keepdims=True)
        sc = p_ref[0]
        blk = (lax.broadcasted_iota(jnp.int32, sc.shape, 0) // PAGES_PER_BLOCK).astype(F32)
        picked = (blk == ids[0][None]) | (blk == ids[1][None]) | (blk == ids[2][None])
        sc = jnp.where(picked, sc, NEG)
        m = jnp.maximum(jnp.max(jnp.max(sc, axis=0), axis=1, keepdims=True), s_new)
        e = jnp.where(picked, jnp.exp(sc - m[None]), 0.0)
        e_new = jnp.exp(s_new - m)
        inv = 1.0 / (jnp.sum(jnp.sum(e, axis=0), axis=1, keepdims=True) + e_new)
        p_ref[0] = e * inv[None]
        pn_ref[0] = jnp.broadcast_to(e_new * inv, (ATTN_HEADS, V7X_LANES))


def _moba_paged_pv_body(n_pages, pt_ref, ids_ref, p_ref, pn_ref, vn_ref, v_hbm, o_ref, vbuf, sem):
    b = pl.program_id(0)
    slot = b % 2
    n_slots = ATTN_HEADS * MOBA_TOPK

    def first_page(seq, s):
        return jnp.maximum(ids_ref[seq * n_slots + s], 0) * PAGES_PER_BLOCK

    def slab_copy(seq, buf, h, r, half):
        page = pt_ref[seq * n_pages + first_page(seq, h * MOBA_TOPK + r) + half]
        rows = pl.ds(h * HEAD_DIM, HEAD_DIM)
        return pltpu.make_async_copy(v_hbm.at[page, rows], vbuf.at[buf, r * PAGES_PER_BLOCK + half, rows],
                                     sem.at[buf])

    def for_all_slabs(fn):
        for h in range(ATTN_HEADS):
            for r in range(MOBA_TOPK):
                for half in range(PAGES_PER_BLOCK):
                    fn(h, r, half)

    @pl.when(b == 0)
    def _():
        for_all_slabs(lambda h, r, half: slab_copy(b, slot, h, r, half).start())

    @pl.when(b + 1 < pl.num_programs(0))
    def _():
        for_all_slabs(lambda h, r, half: slab_copy(b + 1, 1 - slot, h, r, half).start())

    for_all_slabs(lambda h, r, half: slab_copy(b, slot, h, r, half).wait())

    sub = lax.broadcasted_iota(jnp.int32, (ATTN_HEADS, V7X_LANES), 0)
    full = pn_ref[0][:, 0:1] * vn_ref[0]
    for r in range(MOBA_TOPK):
        for half in range(PAGES_PER_BLOCK):
            w = jnp.zeros((ATTN_HEADS, V7X_LANES), F32)
            for h in range(ATTN_HEADS):
                w = jnp.where(sub == h, p_ref[0, first_page(b, h * MOBA_TOPK + r) + half], w)
            full = full + lax.dot_general(w.astype(MXU_DTYPE),
                                          vbuf[slot, r * PAGES_PER_BLOCK + half].astype(MXU_DTYPE), NT_DIMS,
                                          preferred_element_type=F32)
    sub_w = lax.broadcasted_iota(jnp.int32, (ATTN_HEADS, ATTN_W), 0)
    lane_w = lax.broadcasted_iota(jnp.int32, (ATTN_HEADS, ATTN_W), 1)
    o_ref[0] = jnp.sum(jnp.where(lane_w // HEAD_DIM == sub_w, full, 0.0), axis=0, keepdims=True).astype(o_ref.dtype)


def _moba_paged(qs, k_new, v_new, cache_k, cache_v, page_ids):
    nb, n_pages = page_ids.shape
    per = SCORE_PAGES_PER_STEP
    assert n_pages % PAGES_PER_BLOCK == 0 and n_pages // PAGES_PER_BLOCK <= V7X_LANES and n_pages % per == 0
    groups = n_pages // per
    pt = page_ids.reshape(-1)
    row3 = lambda a: a.reshape(nb, 1, ATTN_W)
    page_shape = (ATTN_W, PAGE_SIZE)

    def per_seq(steps_per_seq, *shape):
        return pl.BlockSpec((1,) + shape, lambda t, *_: (t // steps_per_seq,) + (0,) * len(shape))

    probs, ids, p_new = pl.pallas_call(
        functools.partial(_moba_paged_scores_body, n_pages),
        grid_spec=pltpu.PrefetchScalarGridSpec(
            num_scalar_prefetch=1, grid=(nb * groups,),
            in_specs=[per_seq(groups, 1, ATTN_W), per_seq(groups, 1, ATTN_W), pl.BlockSpec(memory_space=pl.ANY)],
            out_specs=[per_seq(groups, n_pages, ATTN_HEADS, V7X_LANES),
                       per_seq(groups, MOBA_TOPK, ATTN_HEADS, V7X_LANES), per_seq(groups, ATTN_HEADS, V7X_LANES)],
            scratch_shapes=[pltpu.VMEM((2, per) + page_shape, F32), pltpu.SemaphoreType.DMA((2,)),
                            pltpu.VMEM((ATTN_HEADS, V7X_LANES), F32)]),
        out_shape=[jax.ShapeDtypeStruct((nb, n_pages, ATTN_HEADS, V7X_LANES), F32),
                   jax.ShapeDtypeStruct((nb, MOBA_TOPK, ATTN_HEADS, V7X_LANES), jnp.int32),
                   jax.ShapeDtypeStruct((nb, ATTN_HEADS, V7X_LANES), F32)],
        compiler_params=_cparams("arbitrary"), name="moba_paged_scores",
    )(pt, row3(qs), row3(k_new), cache_k)
    slot_ids = jnp.transpose(ids[:, :, :, 0], (0, 2, 1)).reshape(-1)
    return pl.pallas_call(
        functools.partial(_moba_paged_pv_body, n_pages),
        grid_spec=pltpu.PrefetchScalarGridSpec(
            num_scalar_prefetch=2, grid=(nb,),
            in_specs=[per_seq(1, n_pages, ATTN_HEADS, V7X_LANES), per_seq(1, ATTN_HEADS, V7X_LANES),
                      per_seq(1, 1, ATTN_W), pl.BlockSpec(memory_space=pl.ANY)],
            out_specs=per_seq(1, 1, ATTN_W),
            scratch_shapes=[pltpu.VMEM((2, MOBA_TOPK * PAGES_PER_BLOCK) + page_shape, F32),
                            pltpu.SemaphoreType.DMA((2,))]),
        out_shape=jax.ShapeDtypeStruct((nb, 1, ATTN_W), MXU_DTYPE),
        compiler_params=_cparams("arbitrary"), name="moba_paged_pv",
    )(pt, slot_ids, probs, p_new, row3(v_new), cache_v).reshape(nb, ATTN_W)


def _gmlp_in_body(seq_mode, tm, *refs):
    if seq_mode:
        x_ref, g_ref, w_ref, lg_ref, lb_ref, ws_ref, bias_ref, act_ref = refs
    else:
        x_ref, g_ref, w_ref, lg_ref, lb_ref, wrow_ref, brow_ref, act_ref, v_ref = refs
    xn = _rms(x_ref[...], g_ref[...]).astype(MXU_DTYPE)
    u = jax.nn.gelu(_mm(xn, w_ref[:, :GMLP_W]))
    v = jax.nn.gelu(_mm(xn, w_ref[:, GMLP_W:]))
    vc = v - jnp.mean(v, axis=-1, keepdims=True)
    v = vc * lax.rsqrt(jnp.mean(vc * vc, axis=-1, keepdims=True) + EPS) * lg_ref[...] + lb_ref[...]
    if not seq_mode:
        v_ref[...] = v
        act_ref[...] = (u * (v * wrow_ref[...] + brow_ref[...])).astype(act_ref.dtype)
        return
    dg = GMLP_W // GMLP_GROUPS
    r = lax.broadcasted_iota(jnp.int32, (GMLP_CHUNK, GMLP_CHUNK), 0)
    c = lax.broadcasted_iota(jnp.int32, (GMLP_CHUNK, GMLP_CHUNK), 1)
    vb = v.astype(MXU_DTYPE)
    for g in range(GMLP_GROUPS):
        ws = jnp.where(c <= r, ws_ref[g], 0.0).astype(MXU_DTYPE)
        cols = slice(g * dg, (g + 1) * dg)
        for ch in range(tm // GMLP_CHUNK):
            rows = slice(ch * GMLP_CHUNK, (ch + 1) * GMLP_CHUNK)
            s = _mm(ws, vb[rows, cols]) + bias_ref[:, cols]
            act_ref[rows, cols] = (u[rows, cols] * s).astype(act_ref.dtype)


def _gmlp_in(x, gain, w_in, ln_g, ln_b, *, seq_mode, w_s=None, bias=None, wrow=None, brow=None):
    t = x.shape[0]
    tm = min(ROW_TILE, t)
    assert t % tm == 0 and (not seq_mode or tm % GMLP_CHUNK == 0)
    vec = _full((1, GMLP_W))
    in_specs = [_rows(tm, D_MODEL), vec, _full(w_in.shape), vec, vec]
    act = jax.ShapeDtypeStruct((t, GMLP_W), MXU_DTYPE)
    if seq_mode:
        in_specs += [_full(w_s.shape), _full(bias.shape)]
        args = [x, gain, w_in, ln_g, ln_b, w_s, bias]
        out_shape, out_specs = act, _rows(tm, GMLP_W)
    else:
        in_specs += [vec, vec]
        args = [x, gain, w_in, ln_g, ln_b, wrow, brow]
        out_shape = [act, jax.ShapeDtypeStruct((t, GMLP_W), F32)]
        out_specs = [_rows(tm, GMLP_W)] * 2
    return pl.pallas_call(
        functools.partial(_gmlp_in_body, seq_mode, tm), grid=(t // tm,), in_specs=in_specs,
        out_specs=out_specs, out_shape=out_shape, compiler_params=_cparams("arbitrary"),
        name="gmlp_in_seq" if seq_mode else "gmlp_in_rows")(*args)


def _mem_kv_body(m_ref, g_ref, wk_ref, wv_ref, gk_ref, k_ref, v_ref, kb_ref, vb_ref):
    mm = _rms(m_ref[...], g_ref[...]).astype(MXU_DTYPE)
    k = _mm(mm, wk_ref[...])
    v = _mm(mm, wv_ref[...])
    k = jnp.concatenate(
        [_rms(k[:, h * MEM_HEAD_DIM:(h + 1) * MEM_HEAD_DIM], gk_ref[...]) for h in range(MEM_HEADS)], axis=1)
    k_ref[...] = k
    v_ref[...] = v
    kb_ref[...] = k.astype(kb_ref.dtype)
    vb_ref[...] = v.astype(vb_ref.dtype)


def _mem_kv(mem, gain, w_k, w_v, g_k):
    m = mem.shape[0]
    f32o = jax.ShapeDtypeStruct((m, D_MODEL), F32)
    b16o = jax.ShapeDtypeStruct((m, D_MODEL), MXU_DTYPE)
    blk = _full((m, D_MODEL))
    return pl.pallas_call(
        _mem_kv_body, grid=(1,),
        in_specs=[blk, _full((1, D_MODEL)), _full(w_k.shape), _full(w_v.shape), _full((1, MEM_HEAD_DIM))],
        out_specs=[blk] * 4, out_shape=[f32o, f32o, b16o, b16o],
        compiler_params=_cparams("arbitrary"), name="mem_kv")(mem, gain, w_k, w_v, g_k)


def _post_mix_body(n_act, shared_mem, *refs):
    h_ref = refs[0]
    act_refs = refs[1:1 + n_act]
    w_refs = refs[1 + n_act:1 + 2 * n_act]
    rest = refs[1 + 2 * n_act:]
    if shared_mem:
        g_ref, wq_ref, gq_ref, mk_ref, mv_ref, wo_ref, o_ref = rest
    else:
        g_ref, wq_ref, gq_ref, h1_ref, q_ref = rest
    h1 = h_ref[...]
    for a_ref, w_ref in zip(act_refs, w_refs):
        h1 = h1 + _mm(a_ref[...], w_ref[...])
    qc = _mm(_rms(h1, g_ref[...]).astype(MXU_DTYPE), wq_ref[...])
    heads = []
    for hd in range(MEM_HEADS):
        cols = slice(hd * MEM_HEAD_DIM, (hd + 1) * MEM_HEAD_DIM)
        qn = _rms(qc[:, cols], gq_ref[...])
        if not shared_mem:
            heads.append(qn)
            continue
        s = lax.dot_general(qn.astype(MXU_DTYPE), mk_ref[:, cols], NT_DIMS,
                            preferred_element_type=F32) * (MEM_HEAD_DIM ** -0.5)
        p = jnp.exp(s - jnp.max(s, axis=1, keepdims=True))
        o = _mm(p.astype(MXU_DTYPE), mv_ref[:, cols]) / jnp.sum(p, axis=1, keepdims=True)
        heads.append(o.astype(MXU_DTYPE))
    cat = jnp.concatenate(heads, axis=1)
    if shared_mem:
        o_ref[...] = h1 + _mm(cat, wo_ref[...])
    else:
        h1_ref[...] = h1
        q_ref[...] = cat


def _post_mix(h, acts, w_outs, gain, w_q, g_q, mem=None, w_o=None):
    t = h.shape[0]
    tm = min(ROW_TILE, t)
    assert t % tm == 0
    shared = mem is not None
    in_specs = [_rows(tm, D_MODEL)] + [_rows(tm, a.shape[1]) for a in acts] + [_full(w.shape) for w in w_outs]
    in_specs += [_full((1, D_MODEL)), _full(w_q.shape), _full((1, MEM_HEAD_DIM))]
    args = [h, *acts, *w_outs, gain, w_q, g_q]
    f32o = jax.ShapeDtypeStruct((t, D_MODEL), F32)
    if shared:
        in_specs += [_full(mem[0].shape), _full(mem[1].shape), _full(w_o.shape)]
        args += [mem[0], mem[1], w_o]
        out_shape, out_specs = f32o, _rows(tm, D_MODEL)
    else:
        out_shape, out_specs = [f32o, f32o], [_rows(tm, D_MODEL)] * 2
    return pl.pallas_call(
        functools.partial(_post_mix_body, len(acts), shared), grid=(t // tm,), in_specs=in_specs,
        out_specs=out_specs, out_shape=out_shape, compiler_params=_cparams("arbitrary"),
        name="post_mix_shared" if shared else "post_mix_rows")(*args)


def _mem_attend_rows_body(q_ref, k_ref, v_ref, o_ref):
    prod = k_ref[0] * q_ref[0]
    outs = []
    for hd in range(MEM_HEADS):
        cols = slice(hd * MEM_HEAD_DIM, (hd + 1) * MEM_HEAD_DIM)
        s = jnp.sum(prod[:, cols], axis=1, keepdims=True) * (MEM_HEAD_DIM ** -0.5)
        p = jnp.exp(s - jnp.max(s, axis=0, keepdims=True))
        o = jnp.sum(p * v_ref[0][:, cols], axis=0, keepdims=True) / jnp.sum(p, axis=0, keepdims=True)
        outs.append(o)
    o_ref[0] = jnp.concatenate(outs, axis=1).astype(o_ref.dtype)


def _mem_attend_rows(qn, mem_k, mem_v, first):
    nb, m = qn.shape[0], mem_k.shape[1]
    row = pl.BlockSpec((1, 1, D_MODEL), lambda b: (b, 0, 0))
    memb = pl.BlockSpec((1, m, D_MODEL), lambda b: (first + b, 0, 0))
    return pl.pallas_call(
        _mem_attend_rows_body, grid=(nb,), in_specs=[row, memb, memb], out_specs=row,
        out_shape=jax.ShapeDtypeStruct((nb, 1, D_MODEL), MXU_DTYPE),
        compiler_params=_cparams("arbitrary"), name="mem_attend_rows",
    )(qn.reshape(nb, 1, D_MODEL), mem_k, mem_v).reshape(nb, D_MODEL)


def _ffn_body(pre_proj, *refs):
    if pre_proj:
        h_ref, o_in_ref, wo_ref, g_ref, w1_ref, w2_ref, out_ref, h_scr, xn_scr, acc = refs
    else:
        h_ref, g_ref, w1_ref, w2_ref, out_ref, h_scr, xn_scr, acc = refs
    f = pl.program_id(1)

    @pl.when(f == 0)
    def _():
        h = h_ref[...]
        if pre_proj:
            h = h + _mm(o_in_ref[...], wo_ref[...])
        h_scr[...] = h
        xn_scr[...] = _rms(h, g_ref[...]).astype(xn_scr.dtype)
        acc[...] = jnp.zeros_like(acc)

    a = jnp.maximum(_mm(xn_scr[...], w1_ref[...]), 0.0)
    acc[...] += _mm((a * a).astype(MXU_DTYPE), w2_ref[...])

    @pl.when(f == pl.num_programs(1) - 1)
    def _():
        out_ref[...] = h_scr[...] + acc[...]


def _ffn(h, gain, w1, w2, pre=None):
    t = h.shape[0]
    tm = min(FFN_ROW_TILE, t)
    tf = FFN_COL_TILE
    assert t % tm == 0 and FFN_W % tf == 0
    rows = pl.BlockSpec((tm, D_MODEL), lambda i, f: (i, 0))
    in_specs, args = [rows], [h]
    if pre is not None:
        in_specs += [rows, pl.BlockSpec(pre[1].shape, lambda i, f: (0, 0))]
        args += list(pre)
    in_specs += [pl.BlockSpec((1, D_MODEL), lambda i, f: (0, 0)),
                 pl.BlockSpec((D_MODEL, tf), lambda i, f: (0, f)), pl.BlockSpec((tf, D_MODEL), lambda i, f: (f, 0))]
    args += [gain, w1, w2]
    return pl.pallas_call(
        functools.partial(_ffn_body, pre is not None), grid=(t // tm, FFN_W // tf), in_specs=in_specs,
        out_specs=rows, out_shape=jax.ShapeDtypeStruct((t, D_MODEL), F32),
        scratch_shapes=[pltpu.VMEM((tm, D_MODEL), F32), pltpu.VMEM((tm, D_MODEL), MXU_DTYPE),
                        pltpu.VMEM((tm, D_MODEL), F32)],
        compiler_params=_cparams("arbitrary", "arbitrary"), name="ffn")(*args)


def kernel(x_prompt, x_sample, mem_prompt, cache_attn_k, cache_attn_v, state_conv, cache_mem_k, cache_mem_v,
           page_table, norm_mix, norm_cross, norm_mem, norm_ffn, mix_w_in, attn_g_q, attn_g_k, conv_w, mix_w_out,
           gmlp_w_in, gmlp_ln_g, gmlp_ln_b, gmlp_w_s, gmlp_b_s, gmlp_w_out, cross_w_q, cross_w_k, cross_w_v,
           cross_w_o, cross_g_q, cross_g_k, ffn_w1, ffn_w2):
    b_p, t_p, _ = x_prompt.shape
    b_s, t_s, _ = x_sample.shape
    assert b_p == 1 and t_s == 1
    depth = norm_mix.shape[0]
    n_pages = page_table.shape[1]
    past_len = n_pages * PAGE_SIZE
    bf = lambda a: a.astype(MXU_DTYPE)
    vec = lambda a: a.reshape(1, -1)

    half = ROT_DIM // 2
    inv = jnp.power(jnp.float32(ROPE_THETA), -jnp.arange(half, dtype=jnp.float32) * (2.0 / ROT_DIM))
    lane = jnp.arange(V7X_LANES) % HEAD_DIM
    inv_lanes = jnp.where(lane < ROT_DIM, inv[lane % half], 0.0).reshape(1, V7X_LANES).astype(F32)
    inv8 = inv.reshape(half, 1)
    hmat = bf(jnp.kron(jnp.eye(ATTN_HEADS, dtype=F32), jnp.ones((HEAD_DIM, HEAD_DIM), F32)))

    hp = x_prompt.reshape(t_p, D_MODEL)
    hs = x_sample.reshape(b_s, D_MODEL)
    ak_p, av_p, ak_s, av_s, cs_p, cs_s, gv_s, mk_p, mv_p = [], [], [], [], [], [], [], [], []
    for layer in range(depth):
        li = layer // 2
        gmix = vec(norm_mix[layer])
        if layer % 2 == 0:
            w_in = bf(mix_w_in[li])
            gq = jnp.tile(attn_g_q[li], ATTN_HEADS)
            gk = vec(jnp.tile(attn_g_k[li], ATTN_HEADS))
            w_out = bf(mix_w_out[li])
            w_outs = [w_out[:ATTN_W], w_out[ATTN_W:]]
            w_rows = jnp.concatenate([w_in[:, ATTN_W:2 * ATTN_W], w_in[:, 3 * ATTN_W:]], axis=1)
            w_t = jnp.concatenate([w_in[:, :ATTN_W], w_in[:, 2 * ATTN_W:3 * ATTN_W]], axis=1).T
            qt, kt, kb, vt, vtb, cmix, kmean, tail = _mix_in_seq(
                hp, gmix, w_rows, w_t, gq.reshape(ATTN_W, 1), gk, hmat, inv_lanes, inv8, conv_w[li])
            kmean = kmean.reshape(-1, ATTN_W)
            kmean_pad = bf(jnp.pad(kmean, ((0, V7X_LANES - kmean.shape[0]), (0, 0))))
            acts_p = [_moba_seq(qt, kb, vtb, kmean_pad), cmix]
            as_cache = lambda a: jnp.transpose(a.reshape(ATTN_HEADS, HEAD_DIM, t_p), (2, 0, 1))[None]
            ak_p.append(as_cache(kt))
            av_p.append(as_cache(vt))
            cs_p.append(tail[8 - (CONV_WIDTH - 1):].reshape(b_p, CONV_WIDTH - 1, CONV_W))

            qs_s, k_s, v_s, cmix_s, u_s = _mix_in_rows(
                hs, gmix, w_in, vec(gq), gk, hmat, inv_lanes, conv_w[li], state_conv[li, :, 0], state_conv[li, :, 1],
                pos=past_len)
            pool = cache_attn_k.shape[1]
            paged = lambda c: jnp.transpose(c, (0, 1, 3, 4, 2)).reshape(-1, ATTN_W, PAGE_SIZE)
            attn_s = _moba_paged(qs_s, k_s, v_s, paged(cache_attn_k), paged(cache_attn_v), page_table + li * pool)
            acts_s = [attn_s, cmix_s]
            ak_s.append(k_s.reshape(b_s, t_s, ATTN_HEADS, HEAD_DIM))
            av_s.append(v_s.reshape(b_s, t_s, ATTN_HEADS, HEAD_DIM))
            cs_s.append(jnp.stack([state_conv[li, :, 1], u_s], axis=1))
        else:
            w_in = bf(gmlp_w_in[li])
            w_outs = [bf(gmlp_w_out[li])]
            lg, lb = vec(gmlp_ln_g[li]), vec(gmlp_ln_b[li])
            dg = GMLP_W // GMLP_GROUPS
            bias = jnp.repeat(gmlp_b_s[li].T, dg, axis=1)
            acts_p = [_gmlp_in(hp, gmix, w_in, lg, lb, seq_mode=True, w_s=gmlp_w_s[li], bias=bias)]
            wrow = vec(jnp.repeat(gmlp_w_s[li][:, 0, 0], dg))
            brow = vec(jnp.repeat(gmlp_b_s[li][:, 0], dg))
            act_s, gv = _gmlp_in(hs, gmix, w_in, lg, lb, seq_mode=False, wrow=wrow, brow=brow)
            acts_s = [act_s]
            gv_s.append(gv.reshape(b_s, t_s, GMLP_W))

        gcross, w_q, g_q = vec(norm_cross[layer]), bf(cross_w_q[layer]), vec(cross_g_q[layer])
        w_o = bf(cross_w_o[layer])
        mk, mv, mkb, mvb = _mem_kv(mem_prompt.reshape(-1, D_MODEL), vec(norm_mem[layer]), bf(cross_w_k[layer]),
                                   bf(cross_w_v[layer]), vec(cross_g_k[layer]))
        mk_p.append(mk.reshape(b_p, -1, MEM_HEADS, MEM_HEAD_DIM))
        mv_p.append(mv.reshape(b_p, -1, MEM_HEADS, MEM_HEAD_DIM))
        gffn, w1, w2 = vec(norm_ffn[layer]), bf(ffn_w1[layer]), bf(ffn_w2[layer])

        hp = _post_mix(hp, acts_p, w_outs, gcross, w_q, g_q, mem=(mkb, mvb), w_o=w_o)
        hp = _ffn(hp, gffn, w1, w2)

        h1_s, qn_s = _post_mix(hs, acts_s, w_outs, gcross, w_q, g_q)
        m_len = cache_mem_k.shape[2]
        o_s = _mem_attend_rows(qn_s, cache_mem_k.reshape(-1, m_len, D_MODEL),
                               cache_mem_v.reshape(-1, m_len, D_MODEL), layer * b_s)
        hs = _ffn(h1_s, gffn, w1, w2, pre=(o_s, w_o))

    return (hp.reshape(b_p, t_p, D_MODEL), hs.reshape(b_s, t_s, D_MODEL), jnp.stack(ak_p), jnp.stack(av_p),
            jnp.stack(ak_s), jnp.stack(av_s), jnp.stack(cs_p), jnp.stack(cs_s), jnp.stack(gv_s),
            jnp.stack(mk_p), jnp.stack(mv_p))
```

```python
import functools
import math

import jax
import jax.numpy as jnp
from jax import lax
from jax.experimental import pallas as pl
from jax.experimental.pallas import tpu as pltpu

D_MODEL = 1024
ATTN_HEADS = 8
HEAD_DIM = 64
ATTN_W = ATTN_HEADS * HEAD_DIM
ROT_DIM = HEAD_DIM // 4
ROPE_THETA = 500000.0
MOBA_BLOCK = 256
MOBA_TOPK = 3
PAGE_SIZE = 128
CONV_W = D_MODEL - ATTN_W
CONV_WIDTH = 3
GMLP_W = D_MODEL
GMLP_GROUPS = 4
GMLP_CHUNK = 128
MEM_HEADS = 4
MEM_HEAD_DIM = D_MODEL // MEM_HEADS
FFN_W = 4 * D_MODEL
EPS = 1e-6

MXU_DTYPE = jnp.bfloat16
V7X_LANES = 128
V7X_VMEM_BYTES = 64 * 1024 * 1024
VMEM_LIMIT = V7X_VMEM_BYTES * 7 // 8
NEG = -1e30
F32 = jnp.float32

ROW_TILE = 512
FFN_ROW_TILE = 1024
FFN_COL_TILE = 1024
PAGES_PER_BLOCK = MOBA_BLOCK // PAGE_SIZE
SCORE_PAGES_PER_STEP = 16
V_AUG_ROWS = HEAD_DIM + 16
NT_DIMS = (((1,), (1,)), ((), ()))
LOG2_SCORE_SCALE = HEAD_DIM ** -0.5 * math.log2(math.e)


def _cparams(*semantics):
    return pltpu.CompilerParams(dimension_semantics=semantics, vmem_limit_bytes=VMEM_LIMIT)


def _full(shape):
    n = len(shape)
    return pl.BlockSpec(shape, lambda *_: (0,) * n)


def _rows(tm, width):
    return pl.BlockSpec((tm, width), lambda i: (i, 0))


def _rms(x, g):
    return x * lax.rsqrt(jnp.mean(x * x, axis=-1, keepdims=True) + EPS) * g


def _mm(a, b):
    return jnp.dot(a, b, preferred_element_type=F32)


def _top_ids(g, lanef):
    ids = []
    for _ in range(MOBA_TOPK):
        mx = jnp.max(g, axis=1, keepdims=True)
        idx = jnp.min(jnp.where(g == mx, lanef, float(V7X_LANES)), axis=1, keepdims=True)
        ids.append(jnp.where(mx > 0.5 * NEG, idx, -1.0))
        g = jnp.where(lanef == idx, NEG, g)
    return ids


def _head_rms_rows(t, g, hm):
    t2 = t * t
    hi = t2.astype(MXU_DTYPE)
    lo = (t2 - hi.astype(F32)).astype(MXU_DTYPE)
    ss = _mm(hi, hm) + _mm(lo, hm)
    return t * lax.rsqrt(ss * (1.0 / HEAD_DIM) + EPS) * g


def _rope_rows(t, pos, inv_lanes):
    ang = pos * inv_lanes
    cs, sn = jnp.cos(ang), jnp.sin(ang)
    lane = lax.broadcasted_iota(jnp.int32, ang.shape, 1) & (HEAD_DIM - 1)
    half = ROT_DIM // 2
    reps = ATTN_W // V7X_LANES
    c_t = jnp.concatenate([jnp.where(lane < ROT_DIM, cs, 1.0)] * reps, axis=1)
    s_up = jnp.concatenate([jnp.where((lane >= half) & (lane < ROT_DIM), sn, 0.0)] * reps, axis=1)
    s_dn = jnp.concatenate([jnp.where(lane < half, -sn, 0.0)] * reps, axis=1)
    return t * c_t + pltpu.roll(t, half, 1) * s_up + pltpu.roll(t, ATTN_W - half, 1) * s_dn


def _conv_taps(u2, u1, u, cw):
    return u2 * cw[0:1, :] + u1 * cw[1:2, :] + u * cw[2:3, :]


def _mix_in_seq_body(tm, x_ref, g_ref, w_ref, wt_ref, gqc_ref, gk_ref, hm_ref, inv_ref, inv8_ref, cw_ref,
                     qt_ref, kt_ref, kb_ref, vt_ref, vtb_ref, cm_ref, kmean_ref, tail_ref, ubuf):
    i = pl.program_id(0)
    xn = _rms(x_ref[...], g_ref[...]).astype(MXU_DTYPE)
    zt = lax.dot_general(wt_ref[...], xn, NT_DIMS, preferred_element_type=F32)
    post = (i * tm + lax.broadcasted_iota(jnp.int32, (1, tm), 1)).astype(F32)
    ang = inv8_ref[...] * post
    cs, sn = jnp.cos(ang), jnp.sin(ang)
    half = ROT_DIM // 2
    pieces = []
    for h in range(ATTN_HEADS):
        t = zt[h * HEAD_DIM:(h + 1) * HEAD_DIM]
        t = t * lax.rsqrt(jnp.mean(t * t, axis=0, keepdims=True) + EPS) * gqc_ref[h * HEAD_DIM:(h + 1) * HEAD_DIM]
        x1, x2 = t[0:half], t[half:ROT_DIM]
        pieces += [x1 * cs - x2 * sn, x2 * cs + x1 * sn, t[ROT_DIM:]]
    qt_ref[...] = (jnp.concatenate(pieces, axis=0) * LOG2_SCORE_SCALE).astype(qt_ref.dtype)
    vt = zt[ATTN_W:]
    vt_ref[...] = vt
    ones = jnp.ones((V_AUG_ROWS - HEAD_DIM, MOBA_BLOCK), vtb_ref.dtype)
    for b in range(tm // MOBA_BLOCK):
        for h in range(ATTN_HEADS):
            vtb_ref[b, h, :HEAD_DIM, :] = vt[h * HEAD_DIM:(h + 1) * HEAD_DIM,
                                             b * MOBA_BLOCK:(b + 1) * MOBA_BLOCK].astype(vtb_ref.dtype)
            vtb_ref[b, h, HEAD_DIM:, :] = ones

    def proj(c):
        return _mm(xn, w_ref[:, c * ATTN_W:(c + 1) * ATTN_W])

    row = lax.broadcasted_iota(jnp.int32, (tm, V7X_LANES), 0)
    k = _rope_rows(_head_rms_rows(proj(0), gk_ref[...], hm_ref[...]), (i * tm + row).astype(F32), inv_ref[...])
    kt_ref[...] = k.T
    kb_ref[...] = k.astype(kb_ref.dtype)
    for b in range(tm // MOBA_BLOCK):
        kmean_ref[0, b:b + 1, :] = jnp.mean(k[b * MOBA_BLOCK:(b + 1) * MOBA_BLOCK], axis=0, keepdims=True)

    bg = proj(1)
    u = proj(2) * proj(3)

    @pl.when(i == 0)
    def _():
        ubuf[0:8, :] = jnp.zeros((8, CONV_W), F32)

    ubuf[8:8 + tm, :] = u
    conv = _conv_taps(ubuf[6:6 + tm, :], ubuf[7:7 + tm, :], u, cw_ref[...])
    ubuf[0:8, :] = u[tm - 8:tm, :]
    tail_ref[...] = u[tm - 8:tm, :]
    cm_ref[...] = (bg * conv).astype(cm_ref.dtype)


def _mix_in_seq(x, gain, w_rows, w_t, gq_col, gk, hmat, inv_lanes, inv8, conv_w):
    t = x.shape[0]
    tm = min(ROW_TILE, t)
    assert t % tm == 0 and tm % MOBA_BLOCK == 0
    n, nb = t // tm, tm // MOBA_BLOCK
    f32t = jax.ShapeDtypeStruct((ATTN_W, t), F32)
    b16o = jax.ShapeDtypeStruct((t, ATTN_W), MXU_DTYPE)
    cols = pl.BlockSpec((ATTN_W, tm), lambda i: (0, i))
    return pl.pallas_call(
        functools.partial(_mix_in_seq_body, tm), grid=(n,),
        in_specs=[_rows(tm, D_MODEL), _full((1, D_MODEL)), _full(w_rows.shape), _full(w_t.shape),
                  _full((ATTN_W, 1)), _full((1, ATTN_W)), _full(hmat.shape), _full((1, V7X_LANES)),
                  _full((ROT_DIM // 2, 1)), _full((CONV_WIDTH, CONV_W))],
        out_specs=[cols, cols, _rows(tm, ATTN_W), cols,
                   pl.BlockSpec((nb, ATTN_HEADS, V_AUG_ROWS, MOBA_BLOCK), lambda i: (i, 0, 0, 0)), _rows(tm, CONV_W),
                   pl.BlockSpec((1, nb, ATTN_W), lambda i: (i, 0, 0)), _full((8, CONV_W))],
        out_shape=[jax.ShapeDtypeStruct((ATTN_W, t), MXU_DTYPE), f32t, b16o, f32t,
                   jax.ShapeDtypeStruct((t // MOBA_BLOCK, ATTN_HEADS, V_AUG_ROWS, MOBA_BLOCK), MXU_DTYPE), b16o,
                   jax.ShapeDtypeStruct((n, nb, ATTN_W), F32), jax.ShapeDtypeStruct((8, CONV_W), F32)],
        scratch_shapes=[pltpu.VMEM((tm + 8, CONV_W), F32)],
        compiler_params=_cparams("arbitrary"), name="mix_in_seq",
    )(x, gain, w_rows, w_t, gq_col, gk, hmat, inv_lanes, inv8, conv_w)


def _mix_in_rows_body(pos, x_ref, g_ref, w_ref, gq_ref, gk_ref, hm_ref, inv_ref, cw_ref, p2_ref, p1_ref,
                      qs_ref, k_ref, v_ref, cm_ref, u_ref):
    xn = _rms(x_ref[...], g_ref[...]).astype(MXU_DTYPE)

    def proj(c):
        return _mm(xn, w_ref[:, c * ATTN_W:(c + 1) * ATTN_W])

    posf = jnp.full((x_ref.shape[0], V7X_LANES), pos, F32)
    q = _rope_rows(_head_rms_rows(proj(0), gq_ref[...], hm_ref[...]), posf, inv_ref[...])
    k = _rope_rows(_head_rms_rows(proj(1), gk_ref[...], hm_ref[...]), posf, inv_ref[...])
    qs_ref[...] = q * (HEAD_DIM ** -0.5)
    k_ref[...] = k
    v_ref[...] = proj(2)
    bg = proj(3)
    u = proj(4) * proj(5)
    u_ref[...] = u
    cm_ref[...] = (bg * _conv_taps(p2_ref[...], p1_ref[...], u, cw_ref[...])).astype(cm_ref.dtype)


def _mix_in_rows(x, gain, w_in, gq, gk, hmat, inv_lanes, conv_w, prev2, prev1, *, pos):
    t = x.shape[0]
    f32o = jax.ShapeDtypeStruct((t, ATTN_W), F32)
    blk = _rows(t, ATTN_W)
    return pl.pallas_call(
        functools.partial(_mix_in_rows_body, pos), grid=(1,),
        in_specs=[_rows(t, D_MODEL), _full((1, D_MODEL)), _full(w_in.shape), _full((1, ATTN_W)), _full((1, ATTN_W)),
                  _full(hmat.shape), _full((1, V7X_LANES)), _full((CONV_WIDTH, CONV_W)), blk, blk],
        out_specs=[blk] * 5,
        out_shape=[f32o, f32o, f32o, jax.ShapeDtypeStruct((t, ATTN_W), MXU_DTYPE), f32o],
        compiler_params=_cparams("arbitrary"), name="mix_in_rows",
    )(x, gain, w_in, gq, gk, hmat, inv_lanes, conv_w, prev2, prev1)


def _moba_seq_body(qt_ref, k_ref, vt_ref, km_ref, o_ref, m_scr, acc_scr, id_scr, s_even, s_odd):
    i = pl.program_id(1)
    blk = MOBA_BLOCK
    last = vt_ref.shape[0] - 1
    qt = qt_ref[...]
    frow = lax.broadcasted_iota(jnp.int32, (V7X_LANES, blk), 0)
    qh = [jnp.where((frow < HEAD_DIM) == (h == 0), qt, jnp.zeros_like(qt)) for h in range(2)]
    brow = frow.astype(F32)
    krow = lax.broadcasted_iota(jnp.int32, (blk, blk), 0)
    qcol = lax.broadcasted_iota(jnp.int32, (blk, blk), 1)

    def keys(j):
        return k_ref[pl.ds(pl.multiple_of(j * blk, blk), blk), :]

    def raw_scores(dst, step):
        for d in range(2):
            kd = keys(jnp.minimum(2 * step + d, last))
            for h in range(2):
                dst[h, d] = _mm(kd, qh[h])

    def consume(src, step):
        j0 = 2 * step
        for h in range(2):
            vs = jnp.concatenate([vt_ref[jnp.minimum(j0 + d, last), h] for d in range(2)], axis=1)
            ss, picked = [], []
            for d in range(2):
                jf = (j0 + d).astype(F32)
                picked.append((id_scr[h, 0:1, :] == jf) | (id_scr[h, 1:2, :] == jf) | (id_scr[h, 2:3, :] == jf))
                ss.append(src[h, d])
            m_prev = m_scr[h][0:1]
            m_new = m_prev
            for d in range(2):
                m_new = jnp.maximum(m_new, jnp.where(picked[d], jnp.max(ss[d], axis=0, keepdims=True), NEG))
            p = jnp.concatenate([jnp.exp2(ss[d] - jnp.where(picked[d], m_new, -NEG)) for d in range(2)],
                                axis=0).astype(MXU_DTYPE)
            acc_scr[h] = jnp.exp2(m_prev - m_new) * acc_scr[h] + _mm(vs, p)
            m_scr[h] = jnp.broadcast_to(m_new, (8, blk))

    k_own = keys(i)
    for h in range(2):
        gate = jnp.where(frow < i, _mm(km_ref[...], qh[h]), NEG)
        for r in range(MOBA_TOPK):
            mx = jnp.max(gate, axis=0, keepdims=True)
            idx = jnp.min(jnp.where(gate == mx, brow, float(V7X_LANES)), axis=0, keepdims=True)
            id_scr[h, r:r + 1, :] = jnp.where(mx > 0.5 * NEG, idx, -1.0)
            gate = jnp.where(brow == idx, NEG, gate)
        s = jnp.where(krow <= qcol, _mm(k_own, qh[h]), NEG)
        m = jnp.max(s, axis=0, keepdims=True)
        m_scr[h] = jnp.broadcast_to(m, (8, blk))
        acc_scr[h] = _mm(vt_ref[i, h], jnp.exp2(s - m).astype(MXU_DTYPE))

    raw_scores(s_even, 0)

    def two_steps(t, carry):
        raw_scores(s_odd, 2 * t + 1)
        consume(s_even, 2 * t)
        raw_scores(s_even, 2 * t + 2)
        consume(s_odd, 2 * t + 1)
        return carry

    lax.fori_loop(0, (i + 3) // 4, two_steps, 0)
    ot = jnp.concatenate([acc_scr[h][:HEAD_DIM] / acc_scr[h][HEAD_DIM:HEAD_DIM + 1] for h in range(2)], axis=0)
    o_ref[...] = ot.T.astype(o_ref.dtype)


def _moba_seq(qt, kb, vtb, kmean_pad):
    t = kb.shape[0]
    nblk = t // MOBA_BLOCK
    assert t % MOBA_BLOCK == 0 and nblk <= V7X_LANES
    pairs = ATTN_W // V7X_LANES
    return pl.pallas_call(
        _moba_seq_body,
        grid=(pairs, nblk),
        in_specs=[pl.BlockSpec((V7X_LANES, MOBA_BLOCK), lambda p, i: (p, i)),
                  pl.BlockSpec((t, V7X_LANES), lambda p, i: (0, p)),
                  pl.BlockSpec((nblk, 2, V_AUG_ROWS, MOBA_BLOCK), lambda p, i: (0, p, 0, 0)),
                  pl.BlockSpec((V7X_LANES, V7X_LANES), lambda p, i: (0, p))],
        out_specs=pl.BlockSpec((MOBA_BLOCK, V7X_LANES), lambda p, i: (i, p)),
        out_shape=jax.ShapeDtypeStruct((t, ATTN_W), MXU_DTYPE),
        scratch_shapes=[pltpu.VMEM((2, 8, MOBA_BLOCK), F32), pltpu.VMEM((2, V_AUG_ROWS, MOBA_BLOCK), F32),
                        pltpu.VMEM((2, 8, MOBA_BLOCK), F32)] + [pltpu.VMEM((2, 2, MOBA_BLOCK, MOBA_BLOCK), F32)] * 2,
        compiler_params=_cparams("arbitrary", "arbitrary"), name="moba_seq")(qt, kb, vtb, kmean_pad)


def _head_rows(q_row):
    sub = lax.broadcasted_iota(jnp.int32, (ATTN_HEADS, ATTN_W), 0)
    lane = lax.broadcasted_iota(jnp.int32, (ATTN_HEADS, ATTN_W), 1)
    return jnp.where(lane // HEAD_DIM == sub, jnp.broadcast_to(q_row, (ATTN_HEADS, ATTN_W)), 0.0)


def _moba_paged_scores_body(n_pages, pt_ref, q_ref, kn_ref, k_hbm, p_ref, ids_ref, pn_ref, kbuf, sem, gsum):
    per = SCORE_PAGES_PER_STEP
    groups = n_pages // per
    t = pl.program_id(0)
    grp = t % groups
    slot = t % 2

    def page_copy(step, buf, g):
        return pltpu.make_async_copy(k_hbm.at[pt_ref[step * per + g]], kbuf.at[buf, g], sem.at[buf])

    @pl.when(t == 0)
    def _():
        for g in range(per):
            page_copy(t, slot, g).start()

    @pl.when(t + 1 < pl.num_programs(0))
    def _():
        for g in range(per):
            page_copy(t + 1, 1 - slot, g).start()

    for g in range(per):
        page_copy(t, slot, g).wait()

    qd = _head_rows(q_ref[0])
    qb = qd.astype(MXU_DTYPE)
    lane = lax.broadcasted_iota(jnp.int32, (ATTN_HEADS, V7X_LANES), 1)

    @pl.when(grp == 0)
    def _():
        gsum[...] = jnp.zeros_like(gsum)

    for g in range(per):
        pg = grp * per + g
        s = _mm(qb, kbuf[slot, g].astype(MXU_DTYPE))
        p_ref[0, pg] = s
        gsum[...] += jnp.where(lane == pg // PAGES_PER_BLOCK, jnp.sum(s, axis=1, keepdims=True), 0.0)

    @pl.when(grp == groups - 1)
    def _():
        n_blocks = n_pages // PAGES_PER_BLOCK
        lanef = lane.astype(F32)
        gate = jnp.where(lane < n_blocks, gsum[...] * (1.0 / MOBA_BLOCK), NEG)
        ids = _top_ids(gate, lanef)
        for r in range(MOBA_TOPK):
            ids_ref[0, r] = jnp.broadcast_to(ids[r], (ATTN_HEADS, V7X_LANES)).astype(jnp.int32)
        s_new = jnp.sum(qd * kn_ref[0], axis=1, keepdims=True)
        sc = p_ref[0]
        blk = (lax.broadcasted_iota(jnp.int32, sc.shape, 0) // PAGES_PER_BLOCK).astype(F32)
        picked = (blk == ids[0][None]) | (blk == ids[1][None]) | (blk == ids[2][None])
        sc = jnp.where(picked, sc, NEG)
        m = jnp.maximum(jnp.max(jnp.max(sc, axis=0), axis=1, keepdims=True), s_new)
        e = jnp.where(picked, jnp.exp(sc - m[None]), 0.0)
        e_new = jnp.exp(s_new - m)
        inv = 1.0 / (jnp.sum(jnp.sum(e, axis=0), axis=1, keepdims=True) + e_new)
        p_ref[0] = e * inv[None]
        pn_ref[0] = jnp.broadcast_to(e_new * inv, (ATTN_HEADS, V7X_LANES))


def _moba_paged_pv_body(n_pages, pt_ref, ids_ref, p_ref, pn_ref, vn_ref, v_hbm, o_ref, vbuf, sem):
    b = pl.program_id(0)
    slot = b % 2
    n_slots = ATTN_HEADS * MOBA_TOPK

    def first_page(seq, s):
        return jnp.maximum(ids_ref[seq * n_slots + s], 0) * PAGES_PER_BLOCK

    def slab_copy(seq, buf, h, r, half):
        page = pt_ref[seq * n_pages + first_page(seq, h * MOBA_TOPK + r) + half]
        rows = pl.ds(h * HEAD_DIM, HEAD_DIM)
        return pltpu.make_async_copy(v_hbm.at[page, rows], vbuf.at[buf, r * PAGES_PER_BLOCK + half, rows],
                                     sem.at[buf])

    def for_all_slabs(fn):
        for h in range(ATTN_HEADS):
            for r in range(MOBA_TOPK):
                for half in range(PAGES_PER_BLOCK):
                    fn(h, r, half)

    @pl.when(b == 0)
    def _():
        for_all_slabs(lambda h, r, half: slab_copy(b, slot, h, r, half).start())

    @pl.when(b + 1 < pl.num_programs(0))
    def _():
        for_all_slabs(lambda h, r, half: slab_copy(b + 1, 1 - slot, h, r, half).start())

    for_all_slabs(lambda h, r, half: slab_copy(b, slot, h, r, half).wait())

    sub = lax.broadcasted_iota(jnp.int32, (ATTN_HEADS, V7X_LANES), 0)
    full = pn_ref[0][:, 0:1] * vn_ref[0]
    for r in range(MOBA_TOPK):
        for half in range(PAGES_PER_BLOCK):
            w = jnp.zeros((ATTN_HEADS, V7X_LANES), F32)
            for h in range(ATTN_HEADS):
                w = jnp.where(sub == h, p_ref[0, first_page(b, h * MOBA_TOPK + r) + half], w)
            full = full + lax.dot_general(w.astype(MXU_DTYPE),
                                          vbuf[slot, r * PAGES_PER_BLOCK + half].astype(MXU_DTYPE), NT_DIMS,
                                          preferred_element_type=F32)
    sub_w = lax.broadcasted_iota(jnp.int32, (ATTN_HEADS, ATTN_W), 0)
    lane_w = lax.broadcasted_iota(jnp.int32, (ATTN_HEADS, ATTN_W), 1)
    o_ref[0] = jnp.sum(jnp.where(lane_w // HEAD_DIM == sub_w, full, 0.0), axis=0, keepdims=True).astype(o_ref.dtype)


def _moba_paged(qs, k_new, v_new, cache_k, cache_v, page_ids):
    nb, n_pages = page_ids.shape
    per = SCORE_PAGES_PER_STEP
    assert n_pages % PAGES_PER_BLOCK == 0 and n_pages // PAGES_PER_BLOCK <= V7X_LANES and n_pages % per == 0
    groups = n_pages // per
    pt = page_ids.reshape(-1)
    row3 = lambda a: a.reshape(nb, 1, ATTN_W)
    page_shape = (ATTN_W, PAGE_SIZE)

    def per_seq(steps_per_seq, *shape):
        return pl.BlockSpec((1,) + shape, lambda t, *_: (t // steps_per_seq,) + (0,) * len(shape))

    probs, ids, p_new = pl.pallas_call(
        functools.partial(_moba_paged_scores_body, n_pages),
        grid_spec=pltpu.PrefetchScalarGridSpec(
            num_scalar_prefetch=1, grid=(nb * groups,),
            in_specs=[per_seq(groups, 1, ATTN_W), per_seq(groups, 1, ATTN_W), pl.BlockSpec(memory_space=pl.ANY)],
            out_specs=[per_seq(groups, n_pages, ATTN_HEADS, V7X_LANES),
                       per_seq(groups, MOBA_TOPK, ATTN_HEADS, V7X_LANES), per_seq(groups, ATTN_HEADS, V7X_LANES)],
            scratch_shapes=[pltpu.VMEM((2, per) + page_shape, F32), pltpu.SemaphoreType.DMA((2,)),
                            pltpu.VMEM((ATTN_HEADS, V7X_LANES), F32)]),
        out_shape=[jax.ShapeDtypeStruct((nb, n_pages, ATTN_HEADS, V7X_LANES), F32),
                   jax.ShapeDtypeStruct((nb, MOBA_TOPK, ATTN_HEADS, V7X_LANES), jnp.int32),
                   jax.ShapeDtypeStruct((nb, ATTN_HEADS, V7X_LANES), F32)],
        compiler_params=_cparams("arbitrary"), name="moba_paged_scores",
    )(pt, row3(qs), row3(k_new), cache_k)
    slot_ids = jnp.transpose(ids[:, :, :, 0], (0, 2, 1)).reshape(-1)
    return pl.pallas_call(
        functools.partial(_moba_paged_pv_body, n_pages),
        grid_spec=pltpu.PrefetchScalarGridSpec(
            num_scalar_prefetch=2, grid=(nb,),
            in_specs=[per_seq(1, n_pages, ATTN_HEADS, V7X_LANES), per_seq(1, ATTN_HEADS, V7X_LANES),
                      per_seq(1, 1, ATTN_W), pl.BlockSpec(memory_space=pl.ANY)],
            out_specs=per_seq(1, 1, ATTN_W),
            scratch_shapes=[pltpu.VMEM((2, MOBA_TOPK * PAGES_PER_BLOCK) + page_shape, F32),
                            pltpu.SemaphoreType.DMA((2,))]),
        out_shape=jax.ShapeDtypeStruct((nb, 1, ATTN_W), MXU_DTYPE),
        compiler_params=_cparams("arbitrary"), name="moba_paged_pv",
    )(pt, slot_ids, probs, p_new, row3(v_new), cache_v).reshape(nb, ATTN_W)


def _gmlp_in_body(seq_mode, tm, *refs):
    if seq_mode:
        x_ref, g_ref, w_ref, lg_ref, lb_ref, ws_ref, bias_ref, act_ref = refs
    else:
        x_ref, g_ref, w_ref, lg_ref, lb_ref, wrow_ref, brow_ref, act_ref, v_ref = refs
    xn = _rms(x_ref[...], g_ref[...]).astype(MXU_DTYPE)
    u = jax.nn.gelu(_mm(xn, w_ref[:, :GMLP_W]))
    v = jax.nn.gelu(_mm(xn, w_ref[:, GMLP_W:]))
    vc = v - jnp.mean(v, axis=-1, keepdims=True)
    v = vc * lax.rsqrt(jnp.mean(vc * vc, axis=-1, keepdims=True) + EPS) * lg_ref[...] + lb_ref[...]
    if not seq_mode:
        v_ref[...] = v
        act_ref[...] = (u * (v * wrow_ref[...] + brow_ref[...])).astype(act_ref.dtype)
        return
    dg = GMLP_W // GMLP_GROUPS
    r = lax.broadcasted_iota(jnp.int32, (GMLP_CHUNK, GMLP_CHUNK), 0)
    c = lax.broadcasted_iota(jnp.int32, (GMLP_CHUNK, GMLP_CHUNK), 1)
    vb = v.astype(MXU_DTYPE)
    for g in range(GMLP_GROUPS):
        ws = jnp.where(c <= r, ws_ref[g], 0.0).astype(MXU_DTYPE)
        cols = slice(g * dg, (g + 1) * dg)
        for ch in range(tm // GMLP_CHUNK):
            rows = slice(ch * GMLP_CHUNK, (ch + 1) * GMLP_CHUNK)
            s = _mm(ws, vb[rows, cols]) + bias_ref[:, cols]
            act_ref[rows, cols] = (u[rows, cols] * s).astype(act_ref.dtype)


def _gmlp_in(x, gain, w_in, ln_g, ln_b, *, seq_mode, w_s=None, bias=None, wrow=None, brow=None):
    t = x.shape[0]
    tm = min(ROW_TILE, t)
    assert t % tm == 0 and (not seq_mode or tm % GMLP_CHUNK == 0)
    vec = _full((1, GMLP_W))
    in_specs = [_rows(tm, D_MODEL), vec, _full(w_in.shape), vec, vec]
    act = jax.ShapeDtypeStruct((t, GMLP_W), MXU_DTYPE)
    if seq_mode:
        in_specs += [_full(w_s.shape), _full(bias.shape)]
        args = [x, gain, w_in, ln_g, ln_b, w_s, bias]
        out_shape, out_specs = act, _rows(tm, GMLP_W)
    else:
        in_specs += [vec, vec]
        args = [x, gain, w_in, ln_g, ln_b, wrow, brow]
        out_shape = [act, jax.ShapeDtypeStruct((t, GMLP_W), F32)]
        out_specs = [_rows(tm, GMLP_W)] * 2
    return pl.pallas_call(
        functools.partial(_gmlp_in_body, seq_mode, tm), grid=(t // tm,), in_specs=in_specs,
        out_specs=out_specs, out_shape=out_shape, compiler_params=_cparams("arbitrary"),
        name="gmlp_in_seq" if seq_mode else "gmlp_in_rows")(*args)


def _mem_kv_body(m_ref, g_ref, wk_ref, wv_ref, gk_ref, k_ref, v_ref, kb_ref, vb_ref):
    mm = _rms(m_ref[...], g_ref[...]).astype(MXU_DTYPE)
    k = _mm(mm, wk_ref[...])
    v = _mm(mm, wv_ref[...])
    k = jnp.concatenate(
        [_rms(k[:, h * MEM_HEAD_DIM:(h + 1) * MEM_HEAD_DIM], gk_ref[...]) for h in range(MEM_HEADS)], axis=1)
    k_ref[...] = k
    v_ref[...] = v
    kb_ref[...] = k.astype(kb_ref.dtype)
    vb_ref[...] = v.astype(vb_ref.dtype)


def _mem_kv(mem, gain, w_k, w_v, g_k):
    m = mem.shape[0]
    f32o = jax.ShapeDtypeStruct((m, D_MODEL), F32)
    b16o = jax.ShapeDtypeStruct((m, D_MODEL), MXU_DTYPE)
    blk = _full((m, D_MODEL))
    return pl.pallas_call(
        _mem_kv_body, grid=(1,),
        in_specs=[blk, _full((1, D_MODEL)), _full(w_k.shape), _full(w_v.shape), _full((1, MEM_HEAD_DIM))],
        out_specs=[blk] * 4, out_shape=[f32o, f32o, b16o, b16o],
        compiler_params=_cparams("arbitrary"), name="mem_kv")(mem, gain, w_k, w_v, g_k)


def _post_mix_body(n_act, shared_mem, *refs):
    h_ref = refs[0]
    act_refs = refs[1:1 + n_act]
    w_refs = refs[1 + n_act:1 + 2 * n_act]
    rest = refs[1 + 2 * n_act:]
    if shared_mem:
        g_ref, wq_ref, gq_ref, mk_ref, mv_ref, wo_ref, o_ref = rest
    else:
        g_ref, wq_ref, gq_ref, h1_ref, q_ref = rest
    h1 = h_ref[...]
    for a_ref, w_ref in zip(act_refs, w_refs):
        h1 = h1 + _mm(a_ref[...], w_ref[...])
    qc = _mm(_rms(h1, g_ref[...]).astype(MXU_DTYPE), wq_ref[...])
    heads = []
    for hd in range(MEM_HEADS):
        cols = slice(hd * MEM_HEAD_DIM, (hd + 1) * MEM_HEAD_DIM)
        qn = _rms(qc[:, cols], gq_ref[...])
        if not shared_mem:
            heads.append(qn)
            continue
        s = lax.dot_general(qn.astype(MXU_DTYPE), mk_ref[:, cols], NT_DIMS,
                            preferred_element_type=F32) * (MEM_HEAD_DIM ** -0.5)
        p = jnp.exp(s - jnp.max(s, axis=1, keepdims=True))
        o = _mm(p.astype(MXU_DTYPE), mv_ref[:, cols]) / jnp.sum(p, axis=1, keepdims=True)
        heads.append(o.astype(MXU_DTYPE))
    cat = jnp.concatenate(heads, axis=1)
    if shared_mem:
        o_ref[...] = h1 + _mm(cat, wo_ref[...])
    else:
        h1_ref[...] = h1
        q_ref[...] = cat


def _post_mix(h, acts, w_outs, gain, w_q, g_q, mem=None, w_o=None):
    t = h.shape[0]
    tm = min(ROW_TILE, t)
    assert t % tm == 0
    shared = mem is not None
    in_specs = [_rows(tm, D_MODEL)] + [_rows(tm, a.shape[1]) for a in acts] + [_full(w.shape) for w in w_outs]
    in_specs += [_full((1, D_MODEL)), _full(w_q.shape), _full((1, MEM_HEAD_DIM))]
    args = [h, *acts, *w_outs, gain, w_q, g_q]
    f32o = jax.ShapeDtypeStruct((t, D_MODEL), F32)
    if shared:
        in_specs += [_full(mem[0].shape), _full(mem[1].shape), _full(w_o.shape)]
        args += [mem[0], mem[1], w_o]
        out_shape, out_specs = f32o, _rows(tm, D_MODEL)
    else:
        out_shape, out_specs = [f32o, f32o], [_rows(tm, D_MODEL)] * 2
    return pl.pallas_call(
        functools.partial(_post_mix_body, len(acts), shared), grid=(t // tm,), in_specs=in_specs,
        out_specs=out_specs, out_shape=out_shape, compiler_params=_cparams("arbitrary"),
        name="post_mix_shared" if shared else "post_mix_rows")(*args)


def _mem_chunks(a):
    lead = a.shape[:-2]
    a = a.reshape(lead + (MEM_HEADS, MEM_HEAD_DIM // V7X_LANES, V7X_LANES))
    return jnp.swapaxes(a, -3, -2).reshape(lead + (MEM_HEADS * MEM_HEAD_DIM // V7X_LANES, V7X_LANES))


def _mem_unchunk(a):
    lead = a.shape[:-2]
    a = a.reshape(lead + (MEM_HEAD_DIM // V7X_LANES, MEM_HEADS, V7X_LANES))
    return jnp.swapaxes(a, -3, -2).reshape(lead + (D_MODEL,))


def _mem_attend_rows_body(q_ref, k_ref, v_ref, o_ref):
    t = k_ref[0] * q_ref[0][None]
    t = t + pltpu.roll(t, MEM_HEADS, 1)
    s = jnp.sum(t, axis=2, keepdims=True) * (MEM_HEAD_DIM ** -0.5)
    p = jnp.exp(s - jnp.max(s, axis=0, keepdims=True))
    o = jnp.sum(p * v_ref[0], axis=0) / jnp.sum(p, axis=0)
    o_ref[0] = o.astype(o_ref.dtype)


def _mem_attend_rows(qn, mem_k, mem_v, first):
    nb, m = qn.shape[0], mem_k.shape[1]
    chunks = mem_k.shape[2:]
    row = pl.BlockSpec((1,) + chunks, lambda b: (b, 0, 0))
    memb = pl.BlockSpec((1, m) + chunks, lambda b: (first + b, 0, 0, 0))
    q8 = _mem_chunks(qn.reshape(nb, MEM_HEADS, MEM_HEAD_DIM))
    o8 = pl.pallas_call(
        _mem_attend_rows_body, grid=(nb,), in_specs=[row, memb, memb], out_specs=row,
        out_shape=jax.ShapeDtypeStruct((nb,) + chunks, F32),
        compiler_params=_cparams("arbitrary"), name="mem_attend_rows")(q8, mem_k, mem_v)
    return _mem_unchunk(o8).astype(MXU_DTYPE)


def _ffn_body(pre_proj, *refs):
    if pre_proj:
        h_ref, o_in_ref, wo_ref, g_ref, w1_ref, w2_ref, out_ref, h_scr, xn_scr, acc = refs
    else:
        h_ref, g_ref, w1_ref, w2_ref, out_ref, h_scr, xn_scr, acc = refs
    f = pl.program_id(1)

    @pl.when(f == 0)
    def _():
        h = h_ref[...]
        if pre_proj:
            h = h + _mm(o_in_ref[...], wo_ref[...])
        h_scr[...] = h
        xn_scr[...] = _rms(h, g_ref[...]).astype(xn_scr.dtype)
        acc[...] = jnp.zeros_like(acc)

    a = jnp.maximum(_mm(xn_scr[...], w1_ref[...]), 0.0)
    acc[...] += _mm((a * a).astype(MXU_DTYPE), w2_ref[...])

    @pl.when(f == pl.num_programs(1) - 1)
    def _():
        out_ref[...] = h_scr[...] + acc[...]


def _ffn(h, gain, w1, w2, pre=None):
    t = h.shape[0]
    tm = min(FFN_ROW_TILE, t)
    tf = FFN_COL_TILE
    assert t % tm == 0 and FFN_W % tf == 0
    rows = pl.BlockSpec((tm, D_MODEL), lambda i, f: (i, 0))
    in_specs, args = [rows], [h]
    if pre is not None:
        in_specs += [rows, pl.BlockSpec(pre[1].shape, lambda i, f: (0, 0))]
        args += list(pre)
    in_specs += [pl.BlockSpec((1, D_MODEL), lambda i, f: (0, 0)),
                 pl.BlockSpec((D_MODEL, tf), lambda i, f: (0, f)), pl.BlockSpec((tf, D_MODEL), lambda i, f: (f, 0))]
    args += [gain, w1, w2]
    return pl.pallas_call(
        functools.partial(_ffn_body, pre is not None), grid=(t // tm, FFN_W // tf), in_specs=in_specs,
        out_specs=rows, out_shape=jax.ShapeDtypeStruct((t, D_MODEL), F32),
        scratch_shapes=[pltpu.VMEM((tm, D_MODEL), F32), pltpu.VMEM((tm, D_MODEL), MXU_DTYPE),
                        pltpu.VMEM((tm, D_MODEL), F32)],
        compiler_params=_cparams("arbitrary", "arbitrary"), name="ffn")(*args)


def kernel(x_prompt, x_sample, mem_prompt, cache_attn_k, cache_attn_v, state_conv, cache_mem_k, cache_mem_v,
           page_table, norm_mix, norm_cross, norm_mem, norm_ffn, mix_w_in, attn_g_q, attn_g_k, conv_w, mix_w_out,
           gmlp_w_in, gmlp_ln_g, gmlp_ln_b, gmlp_w_s, gmlp_b_s, gmlp_w_out, cross_w_q, cross_w_k, cross_w_v,
           cross_w_o, cross_g_q, cross_g_k, ffn_w1, ffn_w2):
    b_p, t_p, _ = x_prompt.shape
    b_s, t_s, _ = x_sample.shape
    assert b_p == 1 and t_s == 1
    depth = norm_mix.shape[0]
    n_pages = page_table.shape[1]
    past_len = n_pages * PAGE_SIZE
    bf = lambda a: a.astype(MXU_DTYPE)
    vec = lambda a: a.reshape(1, -1)

    half = ROT_DIM // 2
    inv = jnp.power(jnp.float32(ROPE_THETA), -jnp.arange(half, dtype=jnp.float32) * (2.0 / ROT_DIM))
    lane = jnp.arange(V7X_LANES) % HEAD_DIM
    inv_lanes = jnp.where(lane < ROT_DIM, inv[lane % half], 0.0).reshape(1, V7X_LANES).astype(F32)
    inv8 = inv.reshape(half, 1)
    hmat = bf(jnp.kron(jnp.eye(ATTN_HEADS, dtype=F32), jnp.ones((HEAD_DIM, HEAD_DIM), F32)))

    hp = x_prompt.reshape(t_p, D_MODEL)
    hs = x_sample.reshape(b_s, D_MODEL)
    ak_p, av_p, ak_s, av_s, cs_p, cs_s, gv_s, mk_p, mv_p = [], [], [], [], [], [], [], [], []
    for layer in range(depth):
        li = layer // 2
        gmix = vec(norm_mix[layer])
        if layer % 2 == 0:
            w_in = bf(mix_w_in[li])
            gq = jnp.tile(attn_g_q[li], ATTN_HEADS)
            gk = vec(jnp.tile(attn_g_k[li], ATTN_HEADS))
            w_out = bf(mix_w_out[li])
            w_outs = [w_out[:ATTN_W], w_out[ATTN_W:]]
            w_rows = jnp.concatenate([w_in[:, ATTN_W:2 * ATTN_W], w_in[:, 3 * ATTN_W:]], axis=1)
            w_t = jnp.concatenate([w_in[:, :ATTN_W], w_in[:, 2 * ATTN_W:3 * ATTN_W]], axis=1).T
            qt, kt, kb, vt, vtb, cmix, kmean, tail = _mix_in_seq(
                hp, gmix, w_rows, w_t, gq.reshape(ATTN_W, 1), gk, hmat, inv_lanes, inv8, conv_w[li])
            kmean = kmean.reshape(-1, ATTN_W)
            kmean_pad = bf(jnp.pad(kmean, ((0, V7X_LANES - kmean.shape[0]), (0, 0))))
            acts_p = [_moba_seq(qt, kb, vtb, kmean_pad), cmix]
            as_cache = lambda a: jnp.transpose(a.reshape(ATTN_HEADS, HEAD_DIM, t_p), (2, 0, 1))[None]
            ak_p.append(as_cache(kt))
            av_p.append(as_cache(vt))
            cs_p.append(tail[8 - (CONV_WIDTH - 1):].reshape(b_p, CONV_WIDTH - 1, CONV_W))

            qs_s, k_s, v_s, cmix_s, u_s = _mix_in_rows(
                hs, gmix, w_in, vec(gq), gk, hmat, inv_lanes, conv_w[li], state_conv[li, :, 0], state_conv[li, :, 1],
                pos=past_len)
            pool = cache_attn_k.shape[1]
            paged = lambda c: jnp.transpose(c, (0, 1, 3, 4, 2)).reshape(-1, ATTN_W, PAGE_SIZE)
            attn_s = _moba_paged(qs_s, k_s, v_s, paged(cache_attn_k), paged(cache_attn_v), page_table + li * pool)
            acts_s = [attn_s, cmix_s]
            ak_s.append(k_s.reshape(b_s, t_s, ATTN_HEADS, HEAD_DIM))
            av_s.append(v_s.reshape(b_s, t_s, ATTN_HEADS, HEAD_DIM))
            cs_s.append(jnp.stack([state_conv[li, :, 1], u_s], axis=1))
        else:
            w_in = bf(gmlp_w_in[li])
            w_outs = [bf(gmlp_w_out[li])]
            lg, lb = vec(gmlp_ln_g[li]), vec(gmlp_ln_b[li])
            dg = GMLP_W // GMLP_GROUPS
            bias = jnp.repeat(gmlp_b_s[li].T, dg, axis=1)
            acts_p = [_gmlp_in(hp, gmix, w_in, lg, lb, seq_mode=True, w_s=gmlp_w_s[li], bias=bias)]
            wrow = vec(jnp.repeat(gmlp_w_s[li][:, 0, 0], dg))
            brow = vec(jnp.repeat(gmlp_b_s[li][:, 0], dg))
            act_s, gv = _gmlp_in(hs, gmix, w_in, lg, lb, seq_mode=False, wrow=wrow, brow=brow)
            acts_s = [act_s]
            gv_s.append(gv.reshape(b_s, t_s, GMLP_W))

        gcross, w_q, g_q = vec(norm_cross[layer]), bf(cross_w_q[layer]), vec(cross_g_q[layer])
        w_o = bf(cross_w_o[layer])
        mk, mv, mkb, mvb = _mem_kv(mem_prompt.reshape(-1, D_MODEL), vec(norm_mem[layer]), bf(cross_w_k[layer]),
                                   bf(cross_w_v[layer]), vec(cross_g_k[layer]))
        mk_p.append(mk.reshape(b_p, -1, MEM_HEADS, MEM_HEAD_DIM))
        mv_p.append(mv.reshape(b_p, -1, MEM_HEADS, MEM_HEAD_DIM))
        gffn, w1, w2 = vec(norm_ffn[layer]), bf(ffn_w1[layer]), bf(ffn_w2[layer])

        hp = _post_mix(hp, acts_p, w_outs, gcross, w_q, g_q, mem=(mkb, mvb), w_o=w_o)
        hp = _ffn(hp, gffn, w1, w2)

        h1_s, qn_s = _post_mix(hs, acts_s, w_outs, gcross, w_q, g_q)
        stored = lambda c: _mem_chunks(c).reshape((-1,) + c.shape[2:3] + (D_MODEL // V7X_LANES, V7X_LANES))
        o_s = _mem_attend_rows(qn_s, stored(cache_mem_k), stored(cache_mem_v), layer * b_s)
        hs = _ffn(h1_s, gffn, w1, w2, pre=(o_s, w_o))

    return (hp.reshape(b_p, t_p, D_MODEL), hs.reshape(b_s, t_s, D_MODEL), jnp.stack(ak_p), jnp.stack(av_p),
            jnp.stack(ak_s), jnp.stack(av_s), jnp.stack(cs_p), jnp.stack(cs_s), jnp.stack(gv_s),
            jnp.stack(mk_p), jnp.stack(mv_p))
```

```python
import functools
import math

import jax
import jax.numpy as jnp
from jax import lax
from jax.experimental import pallas as pl
from jax.experimental.pallas import tpu as pltpu

D_MODEL = 1024
ATTN_HEADS = 8
HEAD_DIM = 64
ATTN_W = ATTN_HEADS * HEAD_DIM
ROT_DIM = HEAD_DIM // 4
ROPE_THETA = 500000.0
MOBA_BLOCK = 256
MOBA_TOPK = 3
PAGE_SIZE = 128
CONV_W = D_MODEL - ATTN_W
CONV_WIDTH = 3
GMLP_W = D_MODEL
GMLP_GROUPS = 4
GMLP_CHUNK = 128
MEM_HEADS = 4
MEM_HEAD_DIM = D_MODEL // MEM_HEADS
FFN_W = 4 * D_MODEL
EPS = 1e-6

MXU_DTYPE = jnp.bfloat16
V7X_LANES = 128
V7X_VMEM_BYTES = 64 * 1024 * 1024
VMEM_LIMIT = V7X_VMEM_BYTES * 7 // 8
NEG = -1e30
F32 = jnp.float32

ROW_TILE = 512
FFN_ROW_TILE = 1024
FFN_COL_TILE = 1024
PAGES_PER_BLOCK = MOBA_BLOCK // PAGE_SIZE
SCORE_PAGES_PER_STEP = 16
SCORE_RING = 3
V_AUG_ROWS = HEAD_DIM + 16
NT_DIMS = (((1,), (1,)), ((), ()))
LOG2_SCORE_SCALE = HEAD_DIM ** -0.5 * math.log2(math.e)


def _cparams(*semantics):
    return pltpu.CompilerParams(dimension_semantics=semantics, vmem_limit_bytes=VMEM_LIMIT)


def _full(shape):
    n = len(shape)
    return pl.BlockSpec(shape, lambda *_: (0,) * n)


def _rows(tm, width):
    return pl.BlockSpec((tm, width), lambda i: (i, 0))


def _rms(x, g):
    return x * lax.rsqrt(jnp.mean(x * x, axis=-1, keepdims=True) + EPS) * g


def _mm(a, b):
    return jnp.dot(a, b, preferred_element_type=F32)


def _top_ids(g, lanef):
    ids = []
    for _ in range(MOBA_TOPK):
        mx = jnp.max(g, axis=1, keepdims=True)
        idx = jnp.min(jnp.where(g == mx, lanef, float(V7X_LANES)), axis=1, keepdims=True)
        ids.append(jnp.where(mx > 0.5 * NEG, idx, -1.0))
        g = jnp.where(lanef == idx, NEG, g)
    return ids


def _head_rms_rows(t, g, hm):
    t2 = t * t
    hi = t2.astype(MXU_DTYPE)
    lo = (t2 - hi.astype(F32)).astype(MXU_DTYPE)
    ss = _mm(hi, hm) + _mm(lo, hm)
    return t * lax.rsqrt(ss * (1.0 / HEAD_DIM) + EPS) * g


def _rope_rows(t, pos, inv_lanes):
    ang = pos * inv_lanes
    cs, sn = jnp.cos(ang), jnp.sin(ang)
    lane = lax.broadcasted_iota(jnp.int32, ang.shape, 1) & (HEAD_DIM - 1)
    half = ROT_DIM // 2
    reps = ATTN_W // V7X_LANES
    c_t = jnp.concatenate([jnp.where(lane < ROT_DIM, cs, 1.0)] * reps, axis=1)
    s_up = jnp.concatenate([jnp.where((lane >= half) & (lane < ROT_DIM), sn, 0.0)] * reps, axis=1)
    s_dn = jnp.concatenate([jnp.where(lane < half, -sn, 0.0)] * reps, axis=1)
    return t * c_t + pltpu.roll(t, half, 1) * s_up + pltpu.roll(t, ATTN_W - half, 1) * s_dn


def _conv_taps(u2, u1, u, cw):
    return u2 * cw[0:1, :] + u1 * cw[1:2, :] + u * cw[2:3, :]


def _mix_in_seq_body(tm, x_ref, g_ref, w_ref, wt_ref, gqc_ref, gkc_ref, inv8_ref, cw_ref,
                     qt_ref, kt_ref, kb_ref, vt_ref, vtb_ref, cm_ref, kmean_ref, tail_ref, ubuf):
    i = pl.program_id(0)
    xn = _rms(x_ref[...], g_ref[...]).astype(MXU_DTYPE)
    zt = lax.dot_general(wt_ref[...], xn, NT_DIMS, preferred_element_type=F32)
    post = (i * tm + lax.broadcasted_iota(jnp.int32, (1, tm), 1)).astype(F32)
    ang = inv8_ref[...] * post
    cs, sn = jnp.cos(ang), jnp.sin(ang)
    half = ROT_DIM // 2

    def norm_rope(z, gain_ref):
        pieces = []
        for h in range(ATTN_HEADS):
            t = z[h * HEAD_DIM:(h + 1) * HEAD_DIM]
            t = t * lax.rsqrt(jnp.mean(t * t, axis=0, keepdims=True) + EPS) * gain_ref[h * HEAD_DIM:(h + 1) * HEAD_DIM]
            x1, x2 = t[0:half], t[half:ROT_DIM]
            pieces += [x1 * cs - x2 * sn, x2 * cs + x1 * sn, t[ROT_DIM:]]
        return jnp.concatenate(pieces, axis=0)

    qt_ref[...] = (norm_rope(zt[:ATTN_W], gqc_ref) * LOG2_SCORE_SCALE).astype(qt_ref.dtype)
    kt = norm_rope(zt[ATTN_W:2 * ATTN_W], gkc_ref)
    kt_ref[...] = kt
    k = kt.T
    kb_ref[...] = k.astype(kb_ref.dtype)
    for b in range(tm // MOBA_BLOCK):
        kmean_ref[0, b:b + 1, :] = jnp.mean(k[b * MOBA_BLOCK:(b + 1) * MOBA_BLOCK], axis=0, keepdims=True)
    vt = zt[2 * ATTN_W:]
    vt_ref[...] = vt
    ones = jnp.ones((V_AUG_ROWS - HEAD_DIM, MOBA_BLOCK), vtb_ref.dtype)
    for b in range(tm // MOBA_BLOCK):
        for h in range(ATTN_HEADS):
            vtb_ref[b, h, :HEAD_DIM, :] = vt[h * HEAD_DIM:(h + 1) * HEAD_DIM,
                                             b * MOBA_BLOCK:(b + 1) * MOBA_BLOCK].astype(vtb_ref.dtype)
            vtb_ref[b, h, HEAD_DIM:, :] = ones

    def proj(c):
        return _mm(xn, w_ref[:, c * CONV_W:(c + 1) * CONV_W])

    bg = proj(0)
    u = proj(1) * proj(2)

    @pl.when(i == 0)
    def _():
        ubuf[0:8, :] = jnp.zeros((8, CONV_W), F32)

    ubuf[8:8 + tm, :] = u
    conv = _conv_taps(ubuf[6:6 + tm, :], ubuf[7:7 + tm, :], u, cw_ref[...])
    ubuf[0:8, :] = u[tm - 8:tm, :]
    tail_ref[...] = u[tm - 8:tm, :]
    cm_ref[...] = (bg * conv).astype(cm_ref.dtype)


def _mix_in_seq(x, gain, w_rows, w_t, gq_col, gk_col, inv8, conv_w):
    t = x.shape[0]
    tm = min(ROW_TILE, t)
    assert t % tm == 0 and tm % MOBA_BLOCK == 0
    n, nb = t // tm, tm // MOBA_BLOCK
    f32t = jax.ShapeDtypeStruct((ATTN_W, t), F32)
    b16o = jax.ShapeDtypeStruct((t, ATTN_W), MXU_DTYPE)
    cols = pl.BlockSpec((ATTN_W, tm), lambda i: (0, i))
    return pl.pallas_call(
        functools.partial(_mix_in_seq_body, tm), grid=(n,),
        in_specs=[_rows(tm, D_MODEL), _full((1, D_MODEL)), _full(w_rows.shape), _full(w_t.shape),
                  _full((ATTN_W, 1)), _full((ATTN_W, 1)), _full((ROT_DIM // 2, 1)), _full((CONV_WIDTH, CONV_W))],
        out_specs=[cols, cols, _rows(tm, ATTN_W), cols,
                   pl.BlockSpec((nb, ATTN_HEADS, V_AUG_ROWS, MOBA_BLOCK), lambda i: (i, 0, 0, 0)), _rows(tm, CONV_W),
                   pl.BlockSpec((1, nb, ATTN_W), lambda i: (i, 0, 0)), _full((8, CONV_W))],
        out_shape=[jax.ShapeDtypeStruct((ATTN_W, t), MXU_DTYPE), f32t, b16o, f32t,
                   jax.ShapeDtypeStruct((t // MOBA_BLOCK, ATTN_HEADS, V_AUG_ROWS, MOBA_BLOCK), MXU_DTYPE), b16o,
                   jax.ShapeDtypeStruct((n, nb, ATTN_W), F32), jax.ShapeDtypeStruct((8, CONV_W), F32)],
        scratch_shapes=[pltpu.VMEM((tm + 8, CONV_W), F32)],
        compiler_params=_cparams("arbitrary"), name="mix_in_seq",
    )(x, gain, w_rows, w_t, gq_col, gk_col, inv8, conv_w)


def _mix_in_rows_body(pos, x_ref, g_ref, w_ref, gq_ref, gk_ref, hm_ref, inv_ref, cw_ref, p2_ref, p1_ref,
                      qs_ref, k_ref, v_ref, cm_ref, u_ref):
    xn = _rms(x_ref[...], g_ref[...]).astype(MXU_DTYPE)

    def proj(c):
        return _mm(xn, w_ref[:, c * ATTN_W:(c + 1) * ATTN_W])

    posf = jnp.full((x_ref.shape[0], V7X_LANES), pos, F32)
    q = _rope_rows(_head_rms_rows(proj(0), gq_ref[...], hm_ref[...]), posf, inv_ref[...])
    k = _rope_rows(_head_rms_rows(proj(1), gk_ref[...], hm_ref[...]), posf, inv_ref[...])
    qs_ref[...] = q * (HEAD_DIM ** -0.5)
    k_ref[...] = k
    v_ref[...] = proj(2)
    bg = proj(3)
    u = proj(4) * proj(5)
    u_ref[...] = u
    cm_ref[...] = (bg * _conv_taps(p2_ref[...], p1_ref[...], u, cw_ref[...])).astype(cm_ref.dtype)


def _mix_in_rows(x, gain, w_in, gq, gk, hmat, inv_lanes, conv_w, prev2, prev1, *, pos):
    t = x.shape[0]
    f32o = jax.ShapeDtypeStruct((t, ATTN_W), F32)
    blk = _rows(t, ATTN_W)
    return pl.pallas_call(
        functools.partial(_mix_in_rows_body, pos), grid=(1,),
        in_specs=[_rows(t, D_MODEL), _full((1, D_MODEL)), _full(w_in.shape), _full((1, ATTN_W)), _full((1, ATTN_W)),
                  _full(hmat.shape), _full((1, V7X_LANES)), _full((CONV_WIDTH, CONV_W)), blk, blk],
        out_specs=[blk] * 5,
        out_shape=[f32o, f32o, f32o, jax.ShapeDtypeStruct((t, ATTN_W), MXU_DTYPE), f32o],
        compiler_params=_cparams("arbitrary"), name="mix_in_rows",
    )(x, gain, w_in, gq, gk, hmat, inv_lanes, conv_w, prev2, prev1)


def _moba_seq_body(qt_ref, k_ref, vt_ref, km_ref, o_ref, m_scr, acc_scr, id_scr, s_even, s_odd):
    i = pl.program_id(1)
    blk = MOBA_BLOCK
    last = vt_ref.shape[0] - 1
    qt = qt_ref[...]
    frow = lax.broadcasted_iota(jnp.int32, (V7X_LANES, blk), 0)
    qh = [jnp.where((frow < HEAD_DIM) == (h == 0), qt, jnp.zeros_like(qt)) for h in range(2)]
    brow = frow.astype(F32)
    krow = lax.broadcasted_iota(jnp.int32, (blk, blk), 0)
    qcol = lax.broadcasted_iota(jnp.int32, (blk, blk), 1)

    def keys(j):
        return k_ref[pl.ds(pl.multiple_of(j * blk, blk), blk), :]

    def raw_scores(dst, step):
        for d in range(2):
            kd = keys(jnp.minimum(2 * step + d, last))
            for h in range(2):
                dst[h, d] = _mm(kd, qh[h])

    def consume(src, step):
        j0 = 2 * step
        for h in range(2):
            vs = jnp.concatenate([vt_ref[jnp.minimum(j0 + d, last), h] for d in range(2)], axis=1)
            ss, picked = [], []
            for d in range(2):
                jf = (j0 + d).astype(F32)
                picked.append((id_scr[h, 0:1, :] == jf) | (id_scr[h, 1:2, :] == jf) | (id_scr[h, 2:3, :] == jf))
                ss.append(src[h, d])
            m_prev = m_scr[h][0:1]
            m_new = m_prev
            for d in range(2):
                m_new = jnp.maximum(m_new, jnp.where(picked[d], jnp.max(ss[d], axis=0, keepdims=True), NEG))
            p = jnp.concatenate([jnp.exp2(ss[d] - jnp.where(picked[d], m_new, -NEG)) for d in range(2)],
                                axis=0).astype(MXU_DTYPE)
            acc_scr[h] = jnp.exp2(m_prev - m_new) * acc_scr[h] + _mm(vs, p)
            m_scr[h] = jnp.broadcast_to(m_new, (8, blk))

    k_own = keys(i)
    for h in range(2):
        gate = jnp.where(frow < i, _mm(km_ref[...], qh[h]), NEG)
        for r in range(MOBA_TOPK):
            mx = jnp.max(gate, axis=0, keepdims=True)
            idx = jnp.min(jnp.where(gate == mx, brow, float(V7X_LANES)), axis=0, keepdims=True)
            id_scr[h, r:r + 1, :] = jnp.where(mx > 0.5 * NEG, idx, -1.0)
            gate = jnp.where(brow == idx, NEG, gate)
        s = jnp.where(krow <= qcol, _mm(k_own, qh[h]), NEG)
        m = jnp.max(s, axis=0, keepdims=True)
        m_scr[h] = jnp.broadcast_to(m, (8, blk))
        acc_scr[h] = _mm(vt_ref[i, h], jnp.exp2(s - m).astype(MXU_DTYPE))

    raw_scores(s_even, 0)

    def two_steps(t, carry):
        raw_scores(s_odd, 2 * t + 1)
        consume(s_even, 2 * t)
        raw_scores(s_even, 2 * t + 2)
        consume(s_odd, 2 * t + 1)
        return carry

    lax.fori_loop(0, (i + 3) // 4, two_steps, 0)
    ot = jnp.concatenate([acc_scr[h][:HEAD_DIM] / acc_scr[h][HEAD_DIM:HEAD_DIM + 1] for h in range(2)], axis=0)
    o_ref[...] = ot.T.astype(o_ref.dtype)


def _moba_seq(qt, kb, vtb, kmean_pad):
    t = kb.shape[0]
    nblk = t // MOBA_BLOCK
    assert t % MOBA_BLOCK == 0 and nblk <= V7X_LANES
    pairs = ATTN_W // V7X_LANES
    return pl.pallas_call(
        _moba_seq_body,
        grid=(pairs, nblk),
        in_specs=[pl.BlockSpec((V7X_LANES, MOBA_BLOCK), lambda p, i: (p, i)),
                  pl.BlockSpec((t, V7X_LANES), lambda p, i: (0, p)),
                  pl.BlockSpec((nblk, 2, V_AUG_ROWS, MOBA_BLOCK), lambda p, i: (0, p, 0, 0)),
                  pl.BlockSpec((V7X_LANES, V7X_LANES), lambda p, i: (0, p))],
        out_specs=pl.BlockSpec((MOBA_BLOCK, V7X_LANES), lambda p, i: (i, p)),
        out_shape=jax.ShapeDtypeStruct((t, ATTN_W), MXU_DTYPE),
        scratch_shapes=[pltpu.VMEM((2, 8, MOBA_BLOCK), F32), pltpu.VMEM((2, V_AUG_ROWS, MOBA_BLOCK), F32),
                        pltpu.VMEM((2, 8, MOBA_BLOCK), F32)] + [pltpu.VMEM((2, 2, MOBA_BLOCK, MOBA_BLOCK), F32)] * 2,
        compiler_params=_cparams("arbitrary", "arbitrary"), name="moba_seq")(qt, kb, vtb, kmean_pad)


def _head_rows(q_row):
    sub = lax.broadcasted_iota(jnp.int32, (ATTN_HEADS, ATTN_W), 0)
    lane = lax.broadcasted_iota(jnp.int32, (ATTN_HEADS, ATTN_W), 1)
    return jnp.where(lane // HEAD_DIM == sub, jnp.broadcast_to(q_row, (ATTN_HEADS, ATTN_W)), 0.0)


def _moba_paged_scores_body(n_pages, pt_ref, q_ref, kn_ref, k_hbm, p_ref, ids_ref, pn_ref, kbuf, sem, gsum):
    per = SCORE_PAGES_PER_STEP
    groups = n_pages // per
    t = pl.program_id(0)
    n_steps = pl.num_programs(0)
    grp = t % groups
    slot = t % SCORE_RING

    def page_copy(step, g):
        buf = step % SCORE_RING
        return pltpu.make_async_copy(k_hbm.at[pt_ref[step * per + g]], kbuf.at[buf, g], sem.at[buf])

    def start_step(step):
        @pl.when(step < n_steps)
        def _():
            for g in range(per):
                page_copy(step, g).start()

    @pl.when(t == 0)
    def _():
        for ahead in range(SCORE_RING - 1):
            start_step(t + ahead)

    start_step(t + SCORE_RING - 1)
    for g in range(per):
        page_copy(t, g).wait()

    qd = _head_rows(q_ref[0])
    qb = qd.astype(MXU_DTYPE)
    lane = lax.broadcasted_iota(jnp.int32, (ATTN_HEADS, V7X_LANES), 1)

    @pl.when(grp == 0)
    def _():
        gsum[...] = jnp.zeros_like(gsum)

    for g in range(per):
        pg = grp * per + g
        s = _mm(qb, kbuf[slot, g].astype(MXU_DTYPE))
        p_ref[0, pg] = s
        gsum[...] += jnp.where(lane == pg // PAGES_PER_BLOCK, jnp.sum(s, axis=1, keepdims=True), 0.0)

    @pl.when(grp == groups - 1)
    def _():
        n_blocks = n_pages // PAGES_PER_BLOCK
        lanef = lane.astype(F32)
        gate = jnp.where(lane < n_blocks, gsum[...] * (1.0 / MOBA_BLOCK), NEG)
        ids = _top_ids(gate, lanef)
        for r in range(MOBA_TOPK):
            ids_ref[0, r] = jnp.broadcast_to(ids[r], (ATTN_HEADS, V7X_LANES)).astype(jnp.int32)
        s_new = jnp.sum(qd * kn_ref[0], axis=1, keepdims=True)
        sc = p_ref[0]
        blk = (lax.broadcasted_iota(jnp.int32, sc.shape, 0) // PAGES_PER_BLOCK).astype(F32)
        picked = (blk == ids[0][None]) | (blk == ids[1][None]) | (blk == ids[2][None])
        sc = jnp.where(picked, sc, NEG)
        m = jnp.maximum(jnp.max(jnp.max(sc, axis=0), axis=1, keepdims=True), s_new)
        e = jnp.where(picked, jnp.exp(sc - m[None]), 0.0)
        e_new = jnp.exp(s_new - m)
        inv = 1.0 / (jnp.sum(jnp.sum(e, axis=0), axis=1, keepdims=True) + e_new)
        p_ref[0] = e * inv[None]
        pn_ref[0] = jnp.broadcast_to(e_new * inv, (ATTN_HEADS, V7X_LANES))


def _moba_paged_pv_body(n_pages, pt_ref, ids_ref, p_ref, pn_ref, vn_ref, v_hbm, o_ref, vbuf, sem):
    b = pl.program_id(0)
    slot = b % 2
    n_slots = ATTN_HEADS * MOBA_TOPK

    def first_page(seq, s):
        return jnp.maximum(ids_ref[seq * n_slots + s], 0) * PAGES_PER_BLOCK

    def slab_copy(seq, buf, h, r, half):
        page = pt_ref[seq * n_pages + first_page(seq, h * MOBA_TOPK + r) + half]
        rows = pl.ds(h * HEAD_DIM, HEAD_DIM)
        return pltpu.make_async_copy(v_hbm.at[page, rows], vbuf.at[buf, r * PAGES_PER_BLOCK + half, rows],
                                     sem.at[buf])

    def for_all_slabs(fn):
        for h in range(ATTN_HEADS):
            for r in range(MOBA_TOPK):
                for half in range(PAGES_PER_BLOCK):
                    fn(h, r, half)

    @pl.when(b == 0)
    def _():
        for_all_slabs(lambda h, r, half: slab_copy(b, slot, h, r, half).start())

    @pl.when(b + 1 < pl.num_programs(0))
    def _():
        for_all_slabs(lambda h, r, half: slab_copy(b + 1, 1 - slot, h, r, half).start())

    for_all_slabs(lambda h, r, half: slab_copy(b, slot, h, r, half).wait())

    sub = lax.broadcasted_iota(jnp.int32, (ATTN_HEADS, V7X_LANES), 0)
    full = pn_ref[0][:, 0:1] * vn_ref[0]
    for r in range(MOBA_TOPK):
        for half in range(PAGES_PER_BLOCK):
            w = jnp.zeros((ATTN_HEADS, V7X_LANES), F32)
            for h in range(ATTN_HEADS):
                w = jnp.where(sub == h, p_ref[0, first_page(b, h * MOBA_TOPK + r) + half], w)
            full = full + lax.dot_general(w.astype(MXU_DTYPE),
                                          vbuf[slot, r * PAGES_PER_BLOCK + half].astype(MXU_DTYPE), NT_DIMS,
                                          preferred_element_type=F32)
    sub_w = lax.broadcasted_iota(jnp.int32, (ATTN_HEADS, ATTN_W), 0)
    lane_w = lax.broadcasted_iota(jnp.int32, (ATTN_HEADS, ATTN_W), 1)
    o_ref[0] = jnp.sum(jnp.where(lane_w // HEAD_DIM == sub_w, full, 0.0), axis=0, keepdims=True).astype(o_ref.dtype)


def _moba_paged(qs, k_new, v_new, cache_k, cache_v, page_ids):
    nb, n_pages = page_ids.shape
    per = SCORE_PAGES_PER_STEP
    assert n_pages % PAGES_PER_BLOCK == 0 and n_pages // PAGES_PER_BLOCK <= V7X_LANES and n_pages % per == 0
    groups = n_pages // per
    pt = page_ids.reshape(-1)
    row3 = lambda a: a.reshape(nb, 1, ATTN_W)
    page_shape = (ATTN_W, PAGE_SIZE)

    def per_seq(steps_per_seq, *shape):
        return pl.BlockSpec((1,) + shape, lambda t, *_: (t // steps_per_seq,) + (0,) * len(shape))

    probs, ids, p_new = pl.pallas_call(
        functools.partial(_moba_paged_scores_body, n_pages),
        grid_spec=pltpu.PrefetchScalarGridSpec(
            num_scalar_prefetch=1, grid=(nb * groups,),
            in_specs=[per_seq(groups, 1, ATTN_W), per_seq(groups, 1, ATTN_W), pl.BlockSpec(memory_space=pl.ANY)],
            out_specs=[per_seq(groups, n_pages, ATTN_HEADS, V7X_LANES),
                       per_seq(groups, MOBA_TOPK, ATTN_HEADS, V7X_LANES), per_seq(groups, ATTN_HEADS, V7X_LANES)],
            scratch_shapes=[pltpu.VMEM((SCORE_RING, per) + page_shape, F32), pltpu.SemaphoreType.DMA((SCORE_RING,)),
                            pltpu.VMEM((ATTN_HEADS, V7X_LANES), F32)]),
        out_shape=[jax.ShapeDtypeStruct((nb, n_pages, ATTN_HEADS, V7X_LANES), F32),
                   jax.ShapeDtypeStruct((nb, MOBA_TOPK, ATTN_HEADS, V7X_LANES), jnp.int32),
                   jax.ShapeDtypeStruct((nb, ATTN_HEADS, V7X_LANES), F32)],
        compiler_params=_cparams("arbitrary"), name="moba_paged_scores",
    )(pt, row3(qs), row3(k_new), cache_k)
    slot_ids = jnp.transpose(ids[:, :, :, 0], (0, 2, 1)).reshape(-1)
    return pl.pallas_call(
        functools.partial(_moba_paged_pv_body, n_pages),
        grid_spec=pltpu.PrefetchScalarGridSpec(
            num_scalar_prefetch=2, grid=(nb,),
            in_specs=[per_seq(1, n_pages, ATTN_HEADS, V7X_LANES), per_seq(1, ATTN_HEADS, V7X_LANES),
                      per_seq(1, 1, ATTN_W), pl.BlockSpec(memory_space=pl.ANY)],
            out_specs=per_seq(1, 1, ATTN_W),
            scratch_shapes=[pltpu.VMEM((2, MOBA_TOPK * PAGES_PER_BLOCK) + page_shape, F32),
                            pltpu.SemaphoreType.DMA((2,))]),
        out_shape=jax.ShapeDtypeStruct((nb, 1, ATTN_W), MXU_DTYPE),
        compiler_params=_cparams("arbitrary"), name="moba_paged_pv",
    )(pt, slot_ids, probs, p_new, row3(v_new), cache_v).reshape(nb, ATTN_W)


def _gmlp_in_body(seq_mode, tm, *refs):
    if seq_mode:
        x_ref, g_ref, w_ref, lg_ref, lb_ref, ws_ref, bias_ref, act_ref = refs
    else:
        x_ref, g_ref, w_ref, lg_ref, lb_ref, wrow_ref, brow_ref, act_ref, v_ref = refs
    xn = _rms(x_ref[...], g_ref[...]).astype(MXU_DTYPE)
    u = jax.nn.gelu(_mm(xn, w_ref[:, :GMLP_W]))
    v = jax.nn.gelu(_mm(xn, w_ref[:, GMLP_W:]))
    vc = v - jnp.mean(v, axis=-1, keepdims=True)
    v = vc * lax.rsqrt(jnp.mean(vc * vc, axis=-1, keepdims=True) + EPS) * lg_ref[...] + lb_ref[...]
    if not seq_mode:
        v_ref[...] = v
        act_ref[...] = (u * (v * wrow_ref[...] + brow_ref[...])).astype(act_ref.dtype)
        return
    dg = GMLP_W // GMLP_GROUPS
    r = lax.broadcasted_iota(jnp.int32, (GMLP_CHUNK, GMLP_CHUNK), 0)
    c = lax.broadcasted_iota(jnp.int32, (GMLP_CHUNK, GMLP_CHUNK), 1)
    vb = v.astype(MXU_DTYPE)
    for g in range(GMLP_GROUPS):
        ws = jnp.where(c <= r, ws_ref[g], 0.0).astype(MXU_DTYPE)
        cols = slice(g * dg, (g + 1) * dg)
        for ch in range(tm // GMLP_CHUNK):
            rows = slice(ch * GMLP_CHUNK, (ch + 1) * GMLP_CHUNK)
            s = _mm(ws, vb[rows, cols]) + bias_ref[:, cols]
            act_ref[rows, cols] = (u[rows, cols] * s).astype(act_ref.dtype)


def _gmlp_in(x, gain, w_in, ln_g, ln_b, *, seq_mode, w_s=None, bias=None, wrow=None, brow=None):
    t = x.shape[0]
    tm = min(ROW_TILE, t)
    assert t % tm == 0 and (not seq_mode or tm % GMLP_CHUNK == 0)
    vec = _full((1, GMLP_W))
    in_specs = [_rows(tm, D_MODEL), vec, _full(w_in.shape), vec, vec]
    act = jax.ShapeDtypeStruct((t, GMLP_W), MXU_DTYPE)
    if seq_mode:
        in_specs += [_full(w_s.shape), _full(bias.shape)]
        args = [x, gain, w_in, ln_g, ln_b, w_s, bias]
        out_shape, out_specs = act, _rows(tm, GMLP_W)
    else:
        in_specs += [vec, vec]
        args = [x, gain, w_in, ln_g, ln_b, wrow, brow]
        out_shape = [act, jax.ShapeDtypeStruct((t, GMLP_W), F32)]
        out_specs = [_rows(tm, GMLP_W)] * 2
    return pl.pallas_call(
        functools.partial(_gmlp_in_body, seq_mode, tm), grid=(t // tm,), in_specs=in_specs,
        out_specs=out_specs, out_shape=out_shape, compiler_params=_cparams("arbitrary"),
        name="gmlp_in_seq" if seq_mode else "gmlp_in_rows")(*args)


def _mem_kv_body(m_ref, g_ref, wk_ref, wv_ref, gk_ref, k_ref, v_ref, kb_ref, vb_ref):
    mm = _rms(m_ref[...], g_ref[...]).astype(MXU_DTYPE)
    k = _mm(mm, wk_ref[...])
    v = _mm(mm, wv_ref[...])
    k = jnp.concatenate(
        [_rms(k[:, h * MEM_HEAD_DIM:(h + 1) * MEM_HEAD_DIM], gk_ref[...]) for h in range(MEM_HEADS)], axis=1)
    k_ref[...] = k
    v_ref[...] = v
    kb_ref[...] = k.astype(kb_ref.dtype)
    vb_ref[...] = v.astype(vb_ref.dtype)


def _mem_kv(mem, gain, w_k, w_v, g_k):
    m = mem.shape[0]
    f32o = jax.ShapeDtypeStruct((m, D_MODEL), F32)
    b16o = jax.ShapeDtypeStruct((m, D_MODEL), MXU_DTYPE)
    blk = _full((m, D_MODEL))
    return pl.pallas_call(
        _mem_kv_body, grid=(1,),
        in_specs=[blk, _full((1, D_MODEL)), _full(w_k.shape), _full(w_v.shape), _full((1, MEM_HEAD_DIM))],
        out_specs=[blk] * 4, out_shape=[f32o, f32o, b16o, b16o],
        compiler_params=_cparams("arbitrary"), name="mem_kv")(mem, gain, w_k, w_v, g_k)


def _post_mix_body(n_act, shared_mem, *refs):
    h_ref = refs[0]
    act_refs = refs[1:1 + n_act]
    w_refs = refs[1 + n_act:1 + 2 * n_act]
    rest = refs[1 + 2 * n_act:]
    if shared_mem:
        g_ref, wq_ref, gq_ref, mk_ref, mv_ref, wo_ref, o_ref = rest
    else:
        g_ref, wq_ref, gq_ref, h1_ref, q_ref = rest
    h1 = h_ref[...]
    for a_ref, w_ref in zip(act_refs, w_refs):
        h1 = h1 + _mm(a_ref[...], w_ref[...])
    qc = _mm(_rms(h1, g_ref[...]).astype(MXU_DTYPE), wq_ref[...])
    heads = []
    for hd in range(MEM_HEADS):
        cols = slice(hd * MEM_HEAD_DIM, (hd + 1) * MEM_HEAD_DIM)
        qn = _rms(qc[:, cols], gq_ref[...])
        if not shared_mem:
            heads.append(qn)
            continue
        s = lax.dot_general(qn.astype(MXU_DTYPE), mk_ref[:, cols], NT_DIMS,
                            preferred_element_type=F32) * (MEM_HEAD_DIM ** -0.5)
        p = jnp.exp(s - jnp.max(s, axis=1, keepdims=True))
        o = _mm(p.astype(MXU_DTYPE), mv_ref[:, cols]) / jnp.sum(p, axis=1, keepdims=True)
        heads.append(o.astype(MXU_DTYPE))
    cat = jnp.concatenate(heads, axis=1)
    if shared_mem:
        o_ref[...] = h1 + _mm(cat, wo_ref[...])
    else:
        h1_ref[...] = h1
        q_ref[...] = cat


def _post_mix(h, acts, w_outs, gain, w_q, g_q, mem=None, w_o=None):
    t = h.shape[0]
    tm = min(ROW_TILE, t)
    assert t % tm == 0
    shared = mem is not None
    in_specs = [_rows(tm, D_MODEL)] + [_rows(tm, a.shape[1]) for a in acts] + [_full(w.shape) for w in w_outs]
    in_specs += [_full((1, D_MODEL)), _full(w_q.shape), _full((1, MEM_HEAD_DIM))]
    args = [h, *acts, *w_outs, gain, w_q, g_q]
    f32o = jax.ShapeDtypeStruct((t, D_MODEL), F32)
    if shared:
        in_specs += [_full(mem[0].shape), _full(mem[1].shape), _full(w_o.shape)]
        args += [mem[0], mem[1], w_o]
        out_shape, out_specs = f32o, _rows(tm, D_MODEL)
    else:
        out_shape, out_specs = [f32o, f32o], [_rows(tm, D_MODEL)] * 2
    return pl.pallas_call(
        functools.partial(_post_mix_body, len(acts), shared), grid=(t // tm,), in_specs=in_specs,
        out_specs=out_specs, out_shape=out_shape, compiler_params=_cparams("arbitrary"),
        name="post_mix_shared" if shared else "post_mix_rows")(*args)


def _mem_chunks(a):
    lead = a.shape[:-2]
    a = a.reshape(lead + (MEM_HEADS, MEM_HEAD_DIM // V7X_LANES, V7X_LANES))
    return jnp.swapaxes(a, -3, -2).reshape(lead + (MEM_HEADS * MEM_HEAD_DIM // V7X_LANES, V7X_LANES))


def _mem_unchunk(a):
    lead = a.shape[:-2]
    a = a.reshape(lead + (MEM_HEAD_DIM // V7X_LANES, MEM_HEADS, V7X_LANES))
    return jnp.swapaxes(a, -3, -2).reshape(lead + (D_MODEL,))


def _mem_attend_rows_body(q_ref, k_ref, v_ref, o_ref):
    t = k_ref[0] * q_ref[0][None]
    t = t + pltpu.roll(t, MEM_HEADS, 1)
    s = jnp.sum(t, axis=2, keepdims=True) * (MEM_HEAD_DIM ** -0.5)
    p = jnp.exp(s - jnp.max(s, axis=0, keepdims=True))
    o = jnp.sum(p * v_ref[0], axis=0) / jnp.sum(p, axis=0)
    o_ref[0] = o.astype(o_ref.dtype)


def _mem_attend_rows(qn, mem_k, mem_v, first):
    nb, m = qn.shape[0], mem_k.shape[1]
    chunks = mem_k.shape[2:]
    row = pl.BlockSpec((1,) + chunks, lambda b: (b, 0, 0))
    memb = pl.BlockSpec((1, m) + chunks, lambda b: (first + b, 0, 0, 0))
    q8 = _mem_chunks(qn.reshape(nb, MEM_HEADS, MEM_HEAD_DIM))
    o8 = pl.pallas_call(
        _mem_attend_rows_body, grid=(nb,), in_specs=[row, memb, memb], out_specs=row,
        out_shape=jax.ShapeDtypeStruct((nb,) + chunks, F32),
        compiler_params=_cparams("arbitrary"), name="mem_attend_rows")(q8, mem_k, mem_v)
    return _mem_unchunk(o8).astype(MXU_DTYPE)


def _ffn_body(pre_proj, *refs):
    if pre_proj:
        h_ref, o_in_ref, wo_ref, g_ref, w1_ref, w2_ref, out_ref, h_scr, xn_scr, acc = refs
    else:
        h_ref, g_ref, w1_ref, w2_ref, out_ref, h_scr, xn_scr, acc = refs
    f = pl.program_id(1)

    @pl.when(f == 0)
    def _():
        h = h_ref[...]
        if pre_proj:
            h = h + _mm(o_in_ref[...], wo_ref[...])
        h_scr[...] = h
        xn_scr[...] = _rms(h, g_ref[...]).astype(xn_scr.dtype)
        acc[...] = jnp.zeros_like(acc)

    a = jnp.maximum(_mm(xn_scr[...], w1_ref[...]), 0.0)
    acc[...] += _mm((a * a).astype(MXU_DTYPE), w2_ref[...])

    @pl.when(f == pl.num_programs(1) - 1)
    def _():
        out_ref[...] = h_scr[...] + acc[...]


def _ffn(h, gain, w1, w2, pre=None):
    t = h.shape[0]
    tm = min(FFN_ROW_TILE, t)
    tf = FFN_COL_TILE
    assert t % tm == 0 and FFN_W % tf == 0
    rows = pl.BlockSpec((tm, D_MODEL), lambda i, f: (i, 0))
    in_specs, args = [rows], [h]
    if pre is not None:
        in_specs += [rows, pl.BlockSpec(pre[1].shape, lambda i, f: (0, 0))]
        args += list(pre)
    in_specs += [pl.BlockSpec((1, D_MODEL), lambda i, f: (0, 0)),
                 pl.BlockSpec((D_MODEL, tf), lambda i, f: (0, f)), pl.BlockSpec((tf, D_MODEL), lambda i, f: (f, 0))]
    args += [gain, w1, w2]
    return pl.pallas_call(
        functools.partial(_ffn_body, pre is not None), grid=(t // tm, FFN_W // tf), in_specs=in_specs,
        out_specs=rows, out_shape=jax.ShapeDtypeStruct((t, D_MODEL), F32),
        scratch_shapes=[pltpu.VMEM((tm, D_MODEL), F32), pltpu.VMEM((tm, D_MODEL), MXU_DTYPE),
                        pltpu.VMEM((tm, D_MODEL), F32)],
        compiler_params=_cparams("arbitrary", "arbitrary"), name="ffn")(*args)


def kernel(x_prompt, x_sample, mem_prompt, cache_attn_k, cache_attn_v, state_conv, cache_mem_k, cache_mem_v,
           page_table, norm_mix, norm_cross, norm_mem, norm_ffn, mix_w_in, attn_g_q, attn_g_k, conv_w, mix_w_out,
           gmlp_w_in, gmlp_ln_g, gmlp_ln_b, gmlp_w_s, gmlp_b_s, gmlp_w_out, cross_w_q, cross_w_k, cross_w_v,
           cross_w_o, cross_g_q, cross_g_k, ffn_w1, ffn_w2):
    b_p, t_p, _ = x_prompt.shape
    b_s, t_s, _ = x_sample.shape
    assert b_p == 1 and t_s == 1
    depth = norm_mix.shape[0]
    n_pages = page_table.shape[1]
    past_len = n_pages * PAGE_SIZE
    bf = lambda a: a.astype(MXU_DTYPE)
    vec = lambda a: a.reshape(1, -1)

    half = ROT_DIM // 2
    inv = jnp.power(jnp.float32(ROPE_THETA), -jnp.arange(half, dtype=jnp.float32) * (2.0 / ROT_DIM))
    lane = jnp.arange(V7X_LANES) % HEAD_DIM
    inv_lanes = jnp.where(lane < ROT_DIM, inv[lane % half], 0.0).reshape(1, V7X_LANES).astype(F32)
    inv8 = inv.reshape(half, 1)
    hmat = bf(jnp.kron(jnp.eye(ATTN_HEADS, dtype=F32), jnp.ones((HEAD_DIM, HEAD_DIM), F32)))

    hp = x_prompt.reshape(t_p, D_MODEL)
    hs = x_sample.reshape(b_s, D_MODEL)
    ak_p, av_p, ak_s, av_s, cs_p, cs_s, gv_s, mk_p, mv_p = [], [], [], [], [], [], [], [], []
    for layer in range(depth):
        li = layer // 2
        gmix = vec(norm_mix[layer])
        if layer % 2 == 0:
            w_in = bf(mix_w_in[li])
            gq = jnp.tile(attn_g_q[li], ATTN_HEADS)
            gk = jnp.tile(attn_g_k[li], ATTN_HEADS)
            w_out = bf(mix_w_out[li])
            w_outs = [w_out[:ATTN_W], w_out[ATTN_W:]]
            qt, kt, kb, vt, vtb, cmix, kmean, tail = _mix_in_seq(
                hp, gmix, w_in[:, 3 * ATTN_W:], w_in[:, :3 * ATTN_W].T, gq.reshape(ATTN_W, 1), gk.reshape(ATTN_W, 1),
                inv8, conv_w[li])
            kmean = kmean.reshape(-1, ATTN_W)
            kmean_pad = bf(jnp.pad(kmean, ((0, V7X_LANES - kmean.shape[0]), (0, 0))))
            acts_p = [_moba_seq(qt, kb, vtb, kmean_pad), cmix]
            as_cache = lambda a: jnp.transpose(a.reshape(ATTN_HEADS, HEAD_DIM, t_p), (2, 0, 1))[None]
            ak_p.append(as_cache(kt))
            av_p.append(as_cache(vt))
            cs_p.append(tail[8 - (CONV_WIDTH - 1):].reshape(b_p, CONV_WIDTH - 1, CONV_W))

            qs_s, k_s, v_s, cmix_s, u_s = _mix_in_rows(
                hs, gmix, w_in, vec(gq), vec(gk), hmat, inv_lanes, conv_w[li], state_conv[li, :, 0], state_conv[li, :, 1],
                pos=past_len)
            pool = cache_attn_k.shape[1]
            paged = lambda c: jnp.transpose(c, (0, 1, 3, 4, 2)).reshape(-1, ATTN_W, PAGE_SIZE)
            attn_s = _moba_paged(qs_s, k_s, v_s, paged(cache_attn_k), paged(cache_attn_v), page_table + li * pool)
            acts_s = [attn_s, cmix_s]
            ak_s.append(k_s.reshape(b_s, t_s, ATTN_HEADS, HEAD_DIM))
            av_s.append(v_s.reshape(b_s, t_s, ATTN_HEADS, HEAD_DIM))
            cs_s.append(jnp.stack([state_conv[li, :, 1], u_s], axis=1))
        else:
            w_in = bf(gmlp_w_in[li])
            w_outs = [bf(gmlp_w_out[li])]
            lg, lb = vec(gmlp_ln_g[li]), vec(gmlp_ln_b[li])
            dg = GMLP_W // GMLP_GROUPS
            bias = jnp.repeat(gmlp_b_s[li].T, dg, axis=1)
            acts_p = [_gmlp_in(hp, gmix, w_in, lg, lb, seq_mode=True, w_s=gmlp_w_s[li], bias=bias)]
            wrow = vec(jnp.repeat(gmlp_w_s[li][:, 0, 0], dg))
            brow = vec(jnp.repeat(gmlp_b_s[li][:, 0], dg))
            act_s, gv = _gmlp_in(hs, gmix, w_in, lg, lb, seq_mode=False, wrow=wrow, brow=brow)
            acts_s = [act_s]
            gv_s.append(gv.reshape(b_s, t_s, GMLP_W))

        gcross, w_q, g_q = vec(norm_cross[layer]), bf(cross_w_q[layer]), vec(cross_g_q[layer])
        w_o = bf(cross_w_o[layer])
        mk, mv, mkb, mvb = _mem_kv(mem_prompt.reshape(-1, D_MODEL), vec(norm_mem[layer]), bf(cross_w_k[layer]),
                                   bf(cross_w_v[layer]), vec(cross_g_k[layer]))
        mk_p.append(mk.reshape(b_p, -1, MEM_HEADS, MEM_HEAD_DIM))
        mv_p.append(mv.reshape(b_p, -1, MEM_HEADS, MEM_HEAD_DIM))
        gffn, w1, w2 = vec(norm_ffn[layer]), bf(ffn_w1[layer]), bf(ffn_w2[layer])

        hp = _post_mix(hp, acts_p, w_outs, gcross, w_q, g_q, mem=(mkb, mvb), w_o=w_o)
        hp = _ffn(hp, gffn, w1, w2)

        h1_s, qn_s = _post_mix(hs, acts_s, w_outs, gcross, w_q, g_q)
        stored = lambda c: _mem_chunks(c).reshape((-1,) + c.shape[2:3] + (D_MODEL // V7X_LANES, V7X_LANES))
        o_s = _mem_attend_rows(qn_s, stored(cache_mem_k), stored(cache_mem_v), layer * b_s)
        hs = _ffn(h1_s, gffn, w1, w2, pre=(o_s, w_o))

    return (hp.reshape(b_p, t_p, D_MODEL), hs.reshape(b_s, t_s, D_MODEL), jnp.stack(ak_p), jnp.stack(av_p),
            jnp.stack(ak_s), jnp.stack(av_s), jnp.stack(cs_p), jnp.stack(cs_s), jnp.stack(gv_s),
            jnp.stack(mk_p), jnp.stack(mv_p))
```

```python
import functools
import math

import jax
import jax.numpy as jnp
from jax import lax
from jax.experimental import pallas as pl
from jax.experimental.pallas import tpu as pltpu

D_MODEL = 1024
ATTN_HEADS = 8
HEAD_DIM = 64
ATTN_W = ATTN_HEADS * HEAD_DIM
ROT_DIM = HEAD_DIM // 4
ROPE_THETA = 500000.0
MOBA_BLOCK = 256
MOBA_TOPK = 3
PAGE_SIZE = 128
CONV_W = D_MODEL - ATTN_W
CONV_WIDTH = 3
GMLP_W = D_MODEL
GMLP_GROUPS = 4
GMLP_CHUNK = 128
MEM_HEADS = 4
MEM_HEAD_DIM = D_MODEL // MEM_HEADS
FFN_W = 4 * D_MODEL
EPS = 1e-6

MXU_DTYPE = jnp.bfloat16
V7X_LANES = 128
V7X_VMEM_BYTES = 64 * 1024 * 1024
VMEM_LIMIT = V7X_VMEM_BYTES * 7 // 8
NEG = -1e30
F32 = jnp.float32

ROW_TILE = 1024
FFN_ROW_TILE = 1024
FFN_COL_TILE = 1024
PAGES_PER_BLOCK = MOBA_BLOCK // PAGE_SIZE
SCORE_PAGES_PER_STEP = 16
MOBA_QUERY_BLOCKS_PER_STEP = 2
SCORE_RING = 3
V_AUG_ROWS = HEAD_DIM + 16
NT_DIMS = (((1,), (1,)), ((), ()))
LOG2_SCORE_SCALE = HEAD_DIM ** -0.5 * math.log2(math.e)


def _cparams(*semantics):
    return pltpu.CompilerParams(dimension_semantics=semantics, vmem_limit_bytes=VMEM_LIMIT)


def _full(shape):
    n = len(shape)
    return pl.BlockSpec(shape, lambda *_: (0,) * n)


def _rows(tm, width):
    return pl.BlockSpec((tm, width), lambda i: (i, 0))


def _rms(x, g):
    return x * lax.rsqrt(jnp.mean(x * x, axis=-1, keepdims=True) + EPS) * g


def _mm(a, b):
    return jnp.dot(a, b, preferred_element_type=F32)


def _top_ids(g, lanef):
    ids = []
    for _ in range(MOBA_TOPK):
        mx = jnp.max(g, axis=1, keepdims=True)
        idx = jnp.min(jnp.where(g == mx, lanef, float(V7X_LANES)), axis=1, keepdims=True)
        ids.append(jnp.where(mx > 0.5 * NEG, idx, -1.0))
        g = jnp.where(lanef == idx, NEG, g)
    return ids


def _head_rms_rows(t, g, hm):
    t2 = t * t
    hi = t2.astype(MXU_DTYPE)
    lo = (t2 - hi.astype(F32)).astype(MXU_DTYPE)
    ss = _mm(hi, hm) + _mm(lo, hm)
    return t * lax.rsqrt(ss * (1.0 / HEAD_DIM) + EPS) * g


def _rope_rows(t, pos, inv_lanes):
    ang = pos * inv_lanes
    cs, sn = jnp.cos(ang), jnp.sin(ang)
    lane = lax.broadcasted_iota(jnp.int32, ang.shape, 1) & (HEAD_DIM - 1)
    half = ROT_DIM // 2
    reps = ATTN_W // V7X_LANES
    c_t = jnp.concatenate([jnp.where(lane < ROT_DIM, cs, 1.0)] * reps, axis=1)
    s_up = jnp.concatenate([jnp.where((lane >= half) & (lane < ROT_DIM), sn, 0.0)] * reps, axis=1)
    s_dn = jnp.concatenate([jnp.where(lane < half, -sn, 0.0)] * reps, axis=1)
    return t * c_t + pltpu.roll(t, half, 1) * s_up + pltpu.roll(t, ATTN_W - half, 1) * s_dn


def _conv_taps(u2, u1, u, cw):
    return u2 * cw[0:1, :] + u1 * cw[1:2, :] + u * cw[2:3, :]


def _mix_in_seq_body(tm, x_ref, g_ref, w_ref, wt_ref, gqc_ref, gkc_ref, inv8_ref, cw_ref,
                     qt_ref, kt_ref, kb_ref, vt_ref, vtb_ref, cm_ref, kmean_ref, tail_ref, ubuf):
    i = pl.program_id(0)
    xn = _rms(x_ref[...], g_ref[...]).astype(MXU_DTYPE)
    zt = lax.dot_general(wt_ref[...], xn, NT_DIMS, preferred_element_type=F32)
    post = (i * tm + lax.broadcasted_iota(jnp.int32, (1, tm), 1)).astype(F32)
    ang = inv8_ref[...] * post
    cs, sn = jnp.cos(ang), jnp.sin(ang)
    half = ROT_DIM // 2

    def norm_rope(z, gain_ref):
        pieces = []
        for h in range(ATTN_HEADS):
            t = z[h * HEAD_DIM:(h + 1) * HEAD_DIM]
            t = t * lax.rsqrt(jnp.mean(t * t, axis=0, keepdims=True) + EPS) * gain_ref[h * HEAD_DIM:(h + 1) * HEAD_DIM]
            x1, x2 = t[0:half], t[half:ROT_DIM]
            pieces += [x1 * cs - x2 * sn, x2 * cs + x1 * sn, t[ROT_DIM:]]
        return jnp.concatenate(pieces, axis=0)

    qt_ref[...] = (norm_rope(zt[:ATTN_W], gqc_ref) * LOG2_SCORE_SCALE).astype(qt_ref.dtype)
    kt = norm_rope(zt[ATTN_W:2 * ATTN_W], gkc_ref)
    kt_ref[...] = kt
    k = kt.T
    kb_ref[...] = k.astype(kb_ref.dtype)
    for b in range(tm // MOBA_BLOCK):
        kmean_ref[0, b:b + 1, :] = jnp.mean(k[b * MOBA_BLOCK:(b + 1) * MOBA_BLOCK], axis=0, keepdims=True)
    vt = zt[2 * ATTN_W:]
    vt_ref[...] = vt
    ones = jnp.ones((V_AUG_ROWS - HEAD_DIM, MOBA_BLOCK), vtb_ref.dtype)
    for b in range(tm // MOBA_BLOCK):
        for h in range(ATTN_HEADS):
            vtb_ref[b, h, :HEAD_DIM, :] = vt[h * HEAD_DIM:(h + 1) * HEAD_DIM,
                                             b * MOBA_BLOCK:(b + 1) * MOBA_BLOCK].astype(vtb_ref.dtype)
            vtb_ref[b, h, HEAD_DIM:, :] = ones

    def proj(c):
        return _mm(xn, w_ref[:, c * CONV_W:(c + 1) * CONV_W])

    bg = proj(0)
    u = proj(1) * proj(2)

    @pl.when(i == 0)
    def _():
        ubuf[0:8, :] = jnp.zeros((8, CONV_W), F32)

    ubuf[8:8 + tm, :] = u
    conv = _conv_taps(ubuf[6:6 + tm, :], ubuf[7:7 + tm, :], u, cw_ref[...])
    ubuf[0:8, :] = u[tm - 8:tm, :]
    tail_ref[...] = u[tm - 8:tm, :]
    cm_ref[...] = (bg * conv).astype(cm_ref.dtype)


def _mix_in_seq(x, gain, w_rows, w_t, gq_col, gk_col, inv8, conv_w):
    t = x.shape[0]
    tm = min(ROW_TILE, t)
    assert t % tm == 0 and tm % MOBA_BLOCK == 0
    n, nb = t // tm, tm // MOBA_BLOCK
    f32t = jax.ShapeDtypeStruct((ATTN_W, t), F32)
    b16o = jax.ShapeDtypeStruct((t, ATTN_W), MXU_DTYPE)
    cols = pl.BlockSpec((ATTN_W, tm), lambda i: (0, i))
    return pl.pallas_call(
        functools.partial(_mix_in_seq_body, tm), grid=(n,),
        in_specs=[_rows(tm, D_MODEL), _full((1, D_MODEL)), _full(w_rows.shape), _full(w_t.shape),
                  _full((ATTN_W, 1)), _full((ATTN_W, 1)), _full((ROT_DIM // 2, 1)), _full((CONV_WIDTH, CONV_W))],
        out_specs=[cols, cols, _rows(tm, ATTN_W), cols,
                   pl.BlockSpec((nb, ATTN_HEADS, V_AUG_ROWS, MOBA_BLOCK), lambda i: (i, 0, 0, 0)), _rows(tm, CONV_W),
                   pl.BlockSpec((1, nb, ATTN_W), lambda i: (i, 0, 0)), _full((8, CONV_W))],
        out_shape=[jax.ShapeDtypeStruct((ATTN_W, t), MXU_DTYPE), f32t, b16o, f32t,
                   jax.ShapeDtypeStruct((t // MOBA_BLOCK, ATTN_HEADS, V_AUG_ROWS, MOBA_BLOCK), MXU_DTYPE), b16o,
                   jax.ShapeDtypeStruct((n, nb, ATTN_W), F32), jax.ShapeDtypeStruct((8, CONV_W), F32)],
        scratch_shapes=[pltpu.VMEM((tm + 8, CONV_W), F32)],
        compiler_params=_cparams("arbitrary"), name="mix_in_seq",
    )(x, gain, w_rows, w_t, gq_col, gk_col, inv8, conv_w)


def _mix_in_rows_body(pos, x_ref, g_ref, w_ref, gq_ref, gk_ref, hm_ref, inv_ref, cw_ref, p2_ref, p1_ref,
                      qs_ref, k_ref, v_ref, cm_ref, u_ref):
    xn = _rms(x_ref[...], g_ref[...]).astype(MXU_DTYPE)

    def proj(c):
        return _mm(xn, w_ref[:, c * ATTN_W:(c + 1) * ATTN_W])

    posf = jnp.full((x_ref.shape[0], V7X_LANES), pos, F32)
    q = _rope_rows(_head_rms_rows(proj(0), gq_ref[...], hm_ref[...]), posf, inv_ref[...])
    k = _rope_rows(_head_rms_rows(proj(1), gk_ref[...], hm_ref[...]), posf, inv_ref[...])
    qs_ref[...] = q * (HEAD_DIM ** -0.5)
    k_ref[...] = k
    v_ref[...] = proj(2)
    bg = proj(3)
    u = proj(4) * proj(5)
    u_ref[...] = u
    cm_ref[...] = (bg * _conv_taps(p2_ref[...], p1_ref[...], u, cw_ref[...])).astype(cm_ref.dtype)


def _mix_in_rows(x, gain, w_in, gq, gk, hmat, inv_lanes, conv_w, prev2, prev1, *, pos):
    t = x.shape[0]
    f32o = jax.ShapeDtypeStruct((t, ATTN_W), F32)
    blk = _rows(t, ATTN_W)
    return pl.pallas_call(
        functools.partial(_mix_in_rows_body, pos), grid=(1,),
        in_specs=[_rows(t, D_MODEL), _full((1, D_MODEL)), _full(w_in.shape), _full((1, ATTN_W)), _full((1, ATTN_W)),
                  _full(hmat.shape), _full((1, V7X_LANES)), _full((CONV_WIDTH, CONV_W)), blk, blk],
        out_specs=[blk] * 5,
        out_shape=[f32o, f32o, f32o, jax.ShapeDtypeStruct((t, ATTN_W), MXU_DTYPE), f32o],
        compiler_params=_cparams("arbitrary"), name="mix_in_rows",
    )(x, gain, w_in, gq, gk, hmat, inv_lanes, conv_w, prev2, prev1)


def _moba_seq_body(qt_ref, k_ref, vt_ref, km_ref, o_ref, m_scr, acc_scr, id_scr, s_even, s_odd):
    for sub in range(MOBA_QUERY_BLOCKS_PER_STEP):
        _moba_query_block(pl.program_id(1) * MOBA_QUERY_BLOCKS_PER_STEP + sub,
                          qt_ref[:, sub * MOBA_BLOCK:(sub + 1) * MOBA_BLOCK], k_ref, vt_ref, km_ref,
                          o_ref.at[sub * MOBA_BLOCK:(sub + 1) * MOBA_BLOCK], m_scr, acc_scr, id_scr, s_even, s_odd)


def _moba_query_block(i, qt, k_ref, vt_ref, km_ref, o_ref, m_scr, acc_scr, id_scr, s_even, s_odd):
    blk = MOBA_BLOCK
    last = vt_ref.shape[0] - 1
    frow = lax.broadcasted_iota(jnp.int32, (V7X_LANES, blk), 0)
    qh = [jnp.where((frow < HEAD_DIM) == (h == 0), qt, jnp.zeros_like(qt)) for h in range(2)]
    brow = frow.astype(F32)
    krow = lax.broadcasted_iota(jnp.int32, (blk, blk), 0)
    qcol = lax.broadcasted_iota(jnp.int32, (blk, blk), 1)

    def keys(j):
        return k_ref[pl.ds(pl.multiple_of(j * blk, blk), blk), :]

    def raw_scores(dst, step):
        for d in range(2):
            kd = keys(jnp.minimum(2 * step + d, last))
            for h in range(2):
                dst[h, d] = _mm(kd, qh[h])

    def consume(src, step):
        j0 = 2 * step
        for h in range(2):
            vs = jnp.concatenate([vt_ref[jnp.minimum(j0 + d, last), h] for d in range(2)], axis=1)
            ss, picked = [], []
            for d in range(2):
                jf = (j0 + d).astype(F32)
                picked.append((id_scr[h, 0:1, :] == jf) | (id_scr[h, 1:2, :] == jf) | (id_scr[h, 2:3, :] == jf))
                ss.append(src[h, d])
            m_prev = m_scr[h][0:1]
            m_new = m_prev
            for d in range(2):
                m_new = jnp.maximum(m_new, jnp.where(picked[d], jnp.max(ss[d], axis=0, keepdims=True), NEG))
            p = jnp.concatenate([jnp.exp2(ss[d] - jnp.where(picked[d], m_new, -NEG)) for d in range(2)],
                                axis=0).astype(MXU_DTYPE)
            acc_scr[h] = jnp.exp2(m_prev - m_new) * acc_scr[h] + _mm(vs, p)
            m_scr[h] = jnp.broadcast_to(m_new, (8, blk))

    k_own = keys(i)
    for h in range(2):
        gate = jnp.where(frow < i, _mm(km_ref[...], qh[h]), NEG)
        for r in range(MOBA_TOPK):
            mx = jnp.max(gate, axis=0, keepdims=True)
            idx = jnp.min(jnp.where(gate == mx, brow, float(V7X_LANES)), axis=0, keepdims=True)
            id_scr[h, r:r + 1, :] = jnp.where(mx > 0.5 * NEG, idx, -1.0)
            gate = jnp.where(brow == idx, NEG, gate)
        s = jnp.where(krow <= qcol, _mm(k_own, qh[h]), NEG)
        m = jnp.max(s, axis=0, keepdims=True)
        m_scr[h] = jnp.broadcast_to(m, (8, blk))
        acc_scr[h] = _mm(vt_ref[i, h], jnp.exp2(s - m).astype(MXU_DTYPE))

    raw_scores(s_even, 0)

    def two_steps(t, carry):
        raw_scores(s_odd, 2 * t + 1)
        consume(s_even, 2 * t)
        raw_scores(s_even, 2 * t + 2)
        consume(s_odd, 2 * t + 1)
        return carry

    lax.fori_loop(0, (i + 3) // 4, two_steps, 0)
    ot = jnp.concatenate([acc_scr[h][:HEAD_DIM] / acc_scr[h][HEAD_DIM:HEAD_DIM + 1] for h in range(2)], axis=0)
    o_ref[...] = ot.T.astype(o_ref.dtype)


def _moba_seq(qt, kb, vtb, kmean_pad):
    t = kb.shape[0]
    nblk = t // MOBA_BLOCK
    per = MOBA_QUERY_BLOCKS_PER_STEP
    assert t % (per * MOBA_BLOCK) == 0 and nblk <= V7X_LANES
    pairs = ATTN_W // V7X_LANES
    return pl.pallas_call(
        _moba_seq_body,
        grid=(pairs, nblk // per),
        in_specs=[pl.BlockSpec((V7X_LANES, per * MOBA_BLOCK), lambda p, i: (p, i)),
                  pl.BlockSpec((t, V7X_LANES), lambda p, i: (0, p)),
                  pl.BlockSpec((nblk, 2, V_AUG_ROWS, MOBA_BLOCK), lambda p, i: (0, p, 0, 0)),
                  pl.BlockSpec((V7X_LANES, V7X_LANES), lambda p, i: (0, p))],
        out_specs=pl.BlockSpec((per * MOBA_BLOCK, V7X_LANES), lambda p, i: (i, p)),
        out_shape=jax.ShapeDtypeStruct((t, ATTN_W), MXU_DTYPE),
        scratch_shapes=[pltpu.VMEM((2, 8, MOBA_BLOCK), F32), pltpu.VMEM((2, V_AUG_ROWS, MOBA_BLOCK), F32),
                        pltpu.VMEM((2, 8, MOBA_BLOCK), F32)] + [pltpu.VMEM((2, 2, MOBA_BLOCK, MOBA_BLOCK), F32)] * 2,
        compiler_params=_cparams("arbitrary", "arbitrary"), name="moba_seq")(qt, kb, vtb, kmean_pad)


def _head_rows(q_row):
    sub = lax.broadcasted_iota(jnp.int32, (ATTN_HEADS, ATTN_W), 0)
    lane = lax.broadcasted_iota(jnp.int32, (ATTN_HEADS, ATTN_W), 1)
    return jnp.where(lane // HEAD_DIM == sub, jnp.broadcast_to(q_row, (ATTN_HEADS, ATTN_W)), 0.0)


def _moba_paged_scores_body(n_pages, pt_ref, q_ref, kn_ref, k_hbm, p_ref, ids_ref, pn_ref, kbuf, sem, gsum):
    per = SCORE_PAGES_PER_STEP
    groups = n_pages // per
    t = pl.program_id(0)
    n_steps = pl.num_programs(0)
    grp = t % groups
    slot = t % SCORE_RING

    def page_copy(step, g):
        buf = step % SCORE_RING
        return pltpu.make_async_copy(k_hbm.at[pt_ref[step * per + g]], kbuf.at[buf, g], sem.at[buf])

    def start_step(step):
        @pl.when(step < n_steps)
        def _():
            for g in range(per):
                page_copy(step, g).start()

    @pl.when(t == 0)
    def _():
        for ahead in range(SCORE_RING - 1):
            start_step(t + ahead)

    start_step(t + SCORE_RING - 1)
    for g in range(per):
        page_copy(t, g).wait()

    qd = _head_rows(q_ref[0])
    qb = qd.astype(MXU_DTYPE)
    lane = lax.broadcasted_iota(jnp.int32, (ATTN_HEADS, V7X_LANES), 1)

    @pl.when(grp == 0)
    def _():
        gsum[...] = jnp.zeros_like(gsum)

    for g in range(per):
        pg = grp * per + g
        s = _mm(qb, kbuf[slot, g].astype(MXU_DTYPE))
        p_ref[0, pg] = s
        gsum[...] += jnp.where(lane == pg // PAGES_PER_BLOCK, jnp.sum(s, axis=1, keepdims=True), 0.0)

    @pl.when(grp == groups - 1)
    def _():
        n_blocks = n_pages // PAGES_PER_BLOCK
        lanef = lane.astype(F32)
        gate = jnp.where(lane < n_blocks, gsum[...] * (1.0 / MOBA_BLOCK), NEG)
        ids = _top_ids(gate, lanef)
        for r in range(MOBA_TOPK):
            ids_ref[0, r] = jnp.broadcast_to(ids[r], (ATTN_HEADS, V7X_LANES)).astype(jnp.int32)
        s_new = jnp.sum(qd * kn_ref[0], axis=1, keepdims=True)
        sc = p_ref[0]
        blk = (lax.broadcasted_iota(jnp.int32, sc.shape, 0) // PAGES_PER_BLOCK).astype(F32)
        picked = (blk == ids[0][None]) | (blk == ids[1][None]) | (blk == ids[2][None])
        sc = jnp.where(picked, sc, NEG)
        m = jnp.maximum(jnp.max(jnp.max(sc, axis=0), axis=1, keepdims=True), s_new)
        e = jnp.where(picked, jnp.exp(sc - m[None]), 0.0)
        e_new = jnp.exp(s_new - m)
        inv = 1.0 / (jnp.sum(jnp.sum(e, axis=0), axis=1, keepdims=True) + e_new)
        p_ref[0] = e * inv[None]
        pn_ref[0] = jnp.broadcast_to(e_new * inv, (ATTN_HEADS, V7X_LANES))


def _moba_paged_pv_body(n_pages, pt_ref, ids_ref, p_ref, pn_ref, vn_ref, v_hbm, o_ref, vbuf, sem):
    b = pl.program_id(0)
    slot = b % 2
    n_slots = ATTN_HEADS * MOBA_TOPK

    def first_page(seq, s):
        return jnp.maximum(ids_ref[seq * n_slots + s], 0) * PAGES_PER_BLOCK

    def slab_copy(seq, buf, h, r, half):
        page = pt_ref[seq * n_pages + first_page(seq, h * MOBA_TOPK + r) + half]
        rows = pl.ds(h * HEAD_DIM, HEAD_DIM)
        return pltpu.make_async_copy(v_hbm.at[page, rows], vbuf.at[buf, r * PAGES_PER_BLOCK + half, rows],
                                     sem.at[buf])

    def for_all_slabs(fn):
        for h in range(ATTN_HEADS):
            for r in range(MOBA_TOPK):
                for half in range(PAGES_PER_BLOCK):
                    fn(h, r, half)

    @pl.when(b == 0)
    def _():
        for_all_slabs(lambda h, r, half: slab_copy(b, slot, h, r, half).start())

    @pl.when(b + 1 < pl.num_programs(0))
    def _():
        for_all_slabs(lambda h, r, half: slab_copy(b + 1, 1 - slot, h, r, half).start())

    for_all_slabs(lambda h, r, half: slab_copy(b, slot, h, r, half).wait())

    sub = lax.broadcasted_iota(jnp.int32, (ATTN_HEADS, V7X_LANES), 0)
    full = pn_ref[0][:, 0:1] * vn_ref[0]
    for r in range(MOBA_TOPK):
        for half in range(PAGES_PER_BLOCK):
            w = jnp.zeros((ATTN_HEADS, V7X_LANES), F32)
            for h in range(ATTN_HEADS):
                w = jnp.where(sub == h, p_ref[0, first_page(b, h * MOBA_TOPK + r) + half], w)
            full = full + lax.dot_general(w.astype(MXU_DTYPE),
                                          vbuf[slot, r * PAGES_PER_BLOCK + half].astype(MXU_DTYPE), NT_DIMS,
                                          preferred_element_type=F32)
    sub_w = lax.broadcasted_iota(jnp.int32, (ATTN_HEADS, ATTN_W), 0)
    lane_w = lax.broadcasted_iota(jnp.int32, (ATTN_HEADS, ATTN_W), 1)
    o_ref[0] = jnp.sum(jnp.where(lane_w // HEAD_DIM == sub_w, full, 0.0), axis=0, keepdims=True).astype(o_ref.dtype)


def _moba_paged(qs, k_new, v_new, cache_k, cache_v, page_ids):
    nb, n_pages = page_ids.shape
    per = SCORE_PAGES_PER_STEP
    assert n_pages % PAGES_PER_BLOCK == 0 and n_pages // PAGES_PER_BLOCK <= V7X_LANES and n_pages % per == 0
    groups = n_pages // per
    pt = page_ids.reshape(-1)
    row3 = lambda a: a.reshape(nb, 1, ATTN_W)
    page_shape = (ATTN_W, PAGE_SIZE)

    def per_seq(steps_per_seq, *shape):
        return pl.BlockSpec((1,) + shape, lambda t, *_: (t // steps_per_seq,) + (0,) * len(shape))

    probs, ids, p_new = pl.pallas_call(
        functools.partial(_moba_paged_scores_body, n_pages),
        grid_spec=pltpu.PrefetchScalarGridSpec(
            num_scalar_prefetch=1, grid=(nb * groups,),
            in_specs=[per_seq(groups, 1, ATTN_W), per_seq(groups, 1, ATTN_W), pl.BlockSpec(memory_space=pl.ANY)],
            out_specs=[per_seq(groups, n_pages, ATTN_HEADS, V7X_LANES),
                       per_seq(groups, MOBA_TOPK, ATTN_HEADS, V7X_LANES), per_seq(groups, ATTN_HEADS, V7X_LANES)],
            scratch_shapes=[pltpu.VMEM((SCORE_RING, per) + page_shape, F32), pltpu.SemaphoreType.DMA((SCORE_RING,)),
                            pltpu.VMEM((ATTN_HEADS, V7X_LANES), F32)]),
        out_shape=[jax.ShapeDtypeStruct((nb, n_pages, ATTN_HEADS, V7X_LANES), F32),
                   jax.ShapeDtypeStruct((nb, MOBA_TOPK, ATTN_HEADS, V7X_LANES), jnp.int32),
                   jax.ShapeDtypeStruct((nb, ATTN_HEADS, V7X_LANES), F32)],
        compiler_params=_cparams("arbitrary"), name="moba_paged_scores",
    )(pt, row3(qs), row3(k_new), cache_k)
    slot_ids = jnp.transpose(ids[:, :, :, 0], (0, 2, 1)).reshape(-1)
    return pl.pallas_call(
        functools.partial(_moba_paged_pv_body, n_pages),
        grid_spec=pltpu.PrefetchScalarGridSpec(
            num_scalar_prefetch=2, grid=(nb,),
            in_specs=[per_seq(1, n_pages, ATTN_HEADS, V7X_LANES), per_seq(1, ATTN_HEADS, V7X_LANES),
                      per_seq(1, 1, ATTN_W), pl.BlockSpec(memory_space=pl.ANY)],
            out_specs=per_seq(1, 1, ATTN_W),
            scratch_shapes=[pltpu.VMEM((2, MOBA_TOPK * PAGES_PER_BLOCK) + page_shape, F32),
                            pltpu.SemaphoreType.DMA((2,))]),
        out_shape=jax.ShapeDtypeStruct((nb, 1, ATTN_W), MXU_DTYPE),
        compiler_params=_cparams("arbitrary"), name="moba_paged_pv",
    )(pt, slot_ids, probs, p_new, row3(v_new), cache_v).reshape(nb, ATTN_W)


def _gmlp_in_body(seq_mode, tm, *refs):
    if seq_mode:
        x_ref, g_ref, w_ref, lg_ref, lb_ref, ws_ref, bias_ref, act_ref = refs
    else:
        x_ref, g_ref, w_ref, lg_ref, lb_ref, wrow_ref, brow_ref, act_ref, v_ref = refs
    xn = _rms(x_ref[...], g_ref[...]).astype(MXU_DTYPE)
    u = jax.nn.gelu(_mm(xn, w_ref[:, :GMLP_W]))
    v = jax.nn.gelu(_mm(xn, w_ref[:, GMLP_W:]))
    vc = v - jnp.mean(v, axis=-1, keepdims=True)
    v = vc * lax.rsqrt(jnp.mean(vc * vc, axis=-1, keepdims=True) + EPS) * lg_ref[...] + lb_ref[...]
    if not seq_mode:
        v_ref[...] = v
        act_ref[...] = (u * (v * wrow_ref[...] + brow_ref[...])).astype(act_ref.dtype)
        return
    dg = GMLP_W // GMLP_GROUPS
    r = lax.broadcasted_iota(jnp.int32, (GMLP_CHUNK, GMLP_CHUNK), 0)
    c = lax.broadcasted_iota(jnp.int32, (GMLP_CHUNK, GMLP_CHUNK), 1)
    vb = v.astype(MXU_DTYPE)
    for g in range(GMLP_GROUPS):
        ws = jnp.where(c <= r, ws_ref[g], 0.0).astype(MXU_DTYPE)
        cols = slice(g * dg, (g + 1) * dg)
        for ch in range(tm // GMLP_CHUNK):
            rows = slice(ch * GMLP_CHUNK, (ch + 1) * GMLP_CHUNK)
            s = _mm(ws, vb[rows, cols]) + bias_ref[:, cols]
            act_ref[rows, cols] = (u[rows, cols] * s).astype(act_ref.dtype)


def _gmlp_in(x, gain, w_in, ln_g, ln_b, *, seq_mode, w_s=None, bias=None, wrow=None, brow=None):
    t = x.shape[0]
    tm = min(ROW_TILE, t)
    assert t % tm == 0 and (not seq_mode or tm % GMLP_CHUNK == 0)
    vec = _full((1, GMLP_W))
    in_specs = [_rows(tm, D_MODEL), vec, _full(w_in.shape), vec, vec]
    act = jax.ShapeDtypeStruct((t, GMLP_W), MXU_DTYPE)
    if seq_mode:
        in_specs += [_full(w_s.shape), _full(bias.shape)]
        args = [x, gain, w_in, ln_g, ln_b, w_s, bias]
        out_shape, out_specs = act, _rows(tm, GMLP_W)
    else:
        in_specs += [vec, vec]
        args = [x, gain, w_in, ln_g, ln_b, wrow, brow]
        out_shape = [act, jax.ShapeDtypeStruct((t, GMLP_W), F32)]
        out_specs = [_rows(tm, GMLP_W)] * 2
    return pl.pallas_call(
        functools.partial(_gmlp_in_body, seq_mode, tm), grid=(t // tm,), in_specs=in_specs,
        out_specs=out_specs, out_shape=out_shape, compiler_params=_cparams("arbitrary"),
        name="gmlp_in_seq" if seq_mode else "gmlp_in_rows")(*args)


def _mem_kv_body(m_ref, g_ref, wk_ref, wv_ref, gk_ref, k_ref, v_ref, kb_ref, vb_ref):
    mm = _rms(m_ref[...], g_ref[...]).astype(MXU_DTYPE)
    k = _mm(mm, wk_ref[...])
    v = _mm(mm, wv_ref[...])
    k = jnp.concatenate(
        [_rms(k[:, h * MEM_HEAD_DIM:(h + 1) * MEM_HEAD_DIM], gk_ref[...]) for h in range(MEM_HEADS)], axis=1)
    k_ref[...] = k
    v_ref[...] = v
    kb_ref[...] = k.astype(kb_ref.dtype)
    vb_ref[...] = v.astype(vb_ref.dtype)


def _mem_kv(mem, gain, w_k, w_v, g_k):
    m = mem.shape[0]
    f32o = jax.ShapeDtypeStruct((m, D_MODEL), F32)
    b16o = jax.ShapeDtypeStruct((m, D_MODEL), MXU_DTYPE)
    blk = _full((m, D_MODEL))
    return pl.pallas_call(
        _mem_kv_body, grid=(1,),
        in_specs=[blk, _full((1, D_MODEL)), _full(w_k.shape), _full(w_v.shape), _full((1, MEM_HEAD_DIM))],
        out_specs=[blk] * 4, out_shape=[f32o, f32o, b16o, b16o],
        compiler_params=_cparams("arbitrary"), name="mem_kv")(mem, gain, w_k, w_v, g_k)


def _post_mix_body(n_act, shared_mem, *refs):
    h_ref = refs[0]
    act_refs = refs[1:1 + n_act]
    w_refs = refs[1 + n_act:1 + 2 * n_act]
    rest = refs[1 + 2 * n_act:]
    if shared_mem:
        g_ref, wq_ref, gq_ref, mk_ref, mv_ref, wo_ref, o_ref = rest
    else:
        g_ref, wq_ref, gq_ref, h1_ref, q_ref = rest
    h1 = h_ref[...]
    for a_ref, w_ref in zip(act_refs, w_refs):
        h1 = h1 + _mm(a_ref[...], w_ref[...])
    qc = _mm(_rms(h1, g_ref[...]).astype(MXU_DTYPE), wq_ref[...])
    heads = []
    for hd in range(MEM_HEADS):
        cols = slice(hd * MEM_HEAD_DIM, (hd + 1) * MEM_HEAD_DIM)
        qn = _rms(qc[:, cols], gq_ref[...])
        if not shared_mem:
            heads.append(qn)
            continue
        s = lax.dot_general(qn.astype(MXU_DTYPE), mk_ref[:, cols], NT_DIMS,
                            preferred_element_type=F32) * (MEM_HEAD_DIM ** -0.5)
        p = jnp.exp(s - jnp.max(s, axis=1, keepdims=True))
        o = _mm(p.astype(MXU_DTYPE), mv_ref[:, cols]) / jnp.sum(p, axis=1, keepdims=True)
        heads.append(o.astype(MXU_DTYPE))
    cat = jnp.concatenate(heads, axis=1)
    if shared_mem:
        o_ref[...] = h1 + _mm(cat, wo_ref[...])
    else:
        h1_ref[...] = h1
        q_ref[...] = cat


def _post_mix(h, acts, w_outs, gain, w_q, g_q, mem=None, w_o=None):
    t = h.shape[0]
    tm = min(ROW_TILE, t)
    assert t % tm == 0
    shared = mem is not None
    in_specs = [_rows(tm, D_MODEL)] + [_rows(tm, a.shape[1]) for a in acts] + [_full(w.shape) for w in w_outs]
    in_specs += [_full((1, D_MODEL)), _full(w_q.shape), _full((1, MEM_HEAD_DIM))]
    args = [h, *acts, *w_outs, gain, w_q, g_q]
    f32o = jax.ShapeDtypeStruct((t, D_MODEL), F32)
    if shared:
        in_specs += [_full(mem[0].shape), _full(mem[1].shape), _full(w_o.shape)]
        args += [mem[0], mem[1], w_o]
        out_shape, out_specs = f32o, _rows(tm, D_MODEL)
    else:
        out_shape, out_specs = [f32o, f32o], [_rows(tm, D_MODEL)] * 2
    return pl.pallas_call(
        functools.partial(_post_mix_body, len(acts), shared), grid=(t // tm,), in_specs=in_specs,
        out_specs=out_specs, out_shape=out_shape, compiler_params=_cparams("arbitrary"),
        name="post_mix_shared" if shared else "post_mix_rows")(*args)


def _mem_chunks(a):
    lead = a.shape[:-2]
    a = a.reshape(lead + (MEM_HEADS, MEM_HEAD_DIM // V7X_LANES, V7X_LANES))
    return jnp.swapaxes(a, -3, -2).reshape(lead + (MEM_HEADS * MEM_HEAD_DIM // V7X_LANES, V7X_LANES))


def _mem_unchunk(a):
    lead = a.shape[:-2]
    a = a.reshape(lead + (MEM_HEAD_DIM // V7X_LANES, MEM_HEADS, V7X_LANES))
    return jnp.swapaxes(a, -3, -2).reshape(lead + (D_MODEL,))


def _mem_attend_rows_body(q_ref, k_ref, v_ref, o_ref):
    t = k_ref[0] * q_ref[0][None]
    t = t + pltpu.roll(t, MEM_HEADS, 1)
    s = jnp.sum(t, axis=2, keepdims=True) * (MEM_HEAD_DIM ** -0.5)
    p = jnp.exp(s - jnp.max(s, axis=0, keepdims=True))
    o = jnp.sum(p * v_ref[0], axis=0) / jnp.sum(p, axis=0)
    o_ref[0] = o.astype(o_ref.dtype)


def _mem_attend_rows(qn, mem_k, mem_v, first):
    nb, m = qn.shape[0], mem_k.shape[1]
    chunks = mem_k.shape[2:]
    row = pl.BlockSpec((1,) + chunks, lambda b: (b, 0, 0))
    memb = pl.BlockSpec((1, m) + chunks, lambda b: (first + b, 0, 0, 0))
    q8 = _mem_chunks(qn.reshape(nb, MEM_HEADS, MEM_HEAD_DIM))
    o8 = pl.pallas_call(
        _mem_attend_rows_body, grid=(nb,), in_specs=[row, memb, memb], out_specs=row,
        out_shape=jax.ShapeDtypeStruct((nb,) + chunks, F32),
        compiler_params=_cparams("arbitrary"), name="mem_attend_rows")(q8, mem_k, mem_v)
    return _mem_unchunk(o8).astype(MXU_DTYPE)


def _ffn_body(pre_proj, *refs):
    if pre_proj:
        h_ref, o_in_ref, wo_ref, g_ref, w1_ref, w2_ref, out_ref, h_scr, xn_scr, acc = refs
    else:
        h_ref, g_ref, w1_ref, w2_ref, out_ref, h_scr, xn_scr, acc = refs
    f = pl.program_id(1)

    @pl.when(f == 0)
    def _():
        h = h_ref[...]
        if pre_proj:
            h = h + _mm(o_in_ref[...], wo_ref[...])
        h_scr[...] = h
        xn_scr[...] = _rms(h, g_ref[...]).astype(xn_scr.dtype)
        acc[...] = jnp.zeros_like(acc)

    a = jnp.maximum(_mm(xn_scr[...], w1_ref[...]), 0.0)
    acc[...] += _mm((a * a).astype(MXU_DTYPE), w2_ref[...])

    @pl.when(f == pl.num_programs(1) - 1)
    def _():
        out_ref[...] = h_scr[...] + acc[...]


def _ffn(h, gain, w1, w2, pre=None):
    t = h.shape[0]
    tm = min(FFN_ROW_TILE, t)
    tf = FFN_COL_TILE
    assert t % tm == 0 and FFN_W % tf == 0
    rows = pl.BlockSpec((tm, D_MODEL), lambda i, f: (i, 0))
    in_specs, args = [rows], [h]
    if pre is not None:
        in_specs += [rows, pl.BlockSpec(pre[1].shape, lambda i, f: (0, 0))]
        args += list(pre)
    in_specs += [pl.BlockSpec((1, D_MODEL), lambda i, f: (0, 0)),
                 pl.BlockSpec((D_MODEL, tf), lambda i, f: (0, f)), pl.BlockSpec((tf, D_MODEL), lambda i, f: (f, 0))]
    args += [gain, w1, w2]
    return pl.pallas_call(
        functools.partial(_ffn_body, pre is not None), grid=(t // tm, FFN_W // tf), in_specs=in_specs,
        out_specs=rows, out_shape=jax.ShapeDtypeStruct((t, D_MODEL), F32),
        scratch_shapes=[pltpu.VMEM((tm, D_MODEL), F32), pltpu.VMEM((tm, D_MODEL), MXU_DTYPE),
                        pltpu.VMEM((tm, D_MODEL), F32)],
        compiler_params=_cparams("arbitrary", "arbitrary"), name="ffn")(*args)


def kernel(x_prompt, x_sample, mem_prompt, cache_attn_k, cache_attn_v, state_conv, cache_mem_k, cache_mem_v,
           page_table, norm_mix, norm_cross, norm_mem, norm_ffn, mix_w_in, attn_g_q, attn_g_k, conv_w, mix_w_out,
           gmlp_w_in, gmlp_ln_g, gmlp_ln_b, gmlp_w_s, gmlp_b_s, gmlp_w_out, cross_w_q, cross_w_k, cross_w_v,
           cross_w_o, cross_g_q, cross_g_k, ffn_w1, ffn_w2):
    b_p, t_p, _ = x_prompt.shape
    b_s, t_s, _ = x_sample.shape
    assert b_p == 1 and t_s == 1
    depth = norm_mix.shape[0]
    n_pages = page_table.shape[1]
    past_len = n_pages * PAGE_SIZE
    bf = lambda a: a.astype(MXU_DTYPE)
    vec = lambda a: a.reshape(1, -1)

    half = ROT_DIM // 2
    inv = jnp.power(jnp.float32(ROPE_THETA), -jnp.arange(half, dtype=jnp.float32) * (2.0 / ROT_DIM))
    lane = jnp.arange(V7X_LANES) % HEAD_DIM
    inv_lanes = jnp.where(lane < ROT_DIM, inv[lane % half], 0.0).reshape(1, V7X_LANES).astype(F32)
    inv8 = inv.reshape(half, 1)
    hmat = bf(jnp.kron(jnp.eye(ATTN_HEADS, dtype=F32), jnp.ones((HEAD_DIM, HEAD_DIM), F32)))

    hp = x_prompt.reshape(t_p, D_MODEL)
    hs = x_sample.reshape(b_s, D_MODEL)
    ak_p, av_p, ak_s, av_s, cs_p, cs_s, gv_s, mk_p, mv_p = [], [], [], [], [], [], [], [], []
    for layer in range(depth):
        li = layer // 2
        gmix = vec(norm_mix[layer])
        if layer % 2 == 0:
            w_in = bf(mix_w_in[li])
            gq = jnp.tile(attn_g_q[li], ATTN_HEADS)
            gk = jnp.tile(attn_g_k[li], ATTN_HEADS)
            w_out = bf(mix_w_out[li])
            w_outs = [w_out[:ATTN_W], w_out[ATTN_W:]]
            qt, kt, kb, vt, vtb, cmix, kmean, tail = _mix_in_seq(
                hp, gmix, w_in[:, 3 * ATTN_W:], w_in[:, :3 * ATTN_W].T, gq.reshape(ATTN_W, 1), gk.reshape(ATTN_W, 1),
                inv8, conv_w[li])
            kmean = kmean.reshape(-1, ATTN_W)
            kmean_pad = bf(jnp.pad(kmean, ((0, V7X_LANES - kmean.shape[0]), (0, 0))))
            acts_p = [_moba_seq(qt, kb, vtb, kmean_pad), cmix]
            as_cache = lambda a: jnp.transpose(a.reshape(ATTN_HEADS, HEAD_DIM, t_p), (2, 0, 1))[None]
            ak_p.append(as_cache(kt))
            av_p.append(as_cache(vt))
            cs_p.append(tail[8 - (CONV_WIDTH - 1):].reshape(b_p, CONV_WIDTH - 1, CONV_W))

            qs_s, k_s, v_s, cmix_s, u_s = _mix_in_rows(
                hs, gmix, w_in, vec(gq), vec(gk), hmat, inv_lanes, conv_w[li], state_conv[li, :, 0], state_conv[li, :, 1],
                pos=past_len)
            pool = cache_attn_k.shape[1]
            paged = lambda c: jnp.transpose(c, (0, 1, 3, 4, 2)).reshape(-1, ATTN_W, PAGE_SIZE)
            attn_s = _moba_paged(qs_s, k_s, v_s, paged(cache_attn_k), paged(cache_attn_v), page_table + li * pool)
            acts_s = [attn_s, cmix_s]
            ak_s.append(k_s.reshape(b_s, t_s, ATTN_HEADS, HEAD_DIM))
            av_s.append(v_s.reshape(b_s, t_s, ATTN_HEADS, HEAD_DIM))
            cs_s.append(jnp.stack([state_conv[li, :, 1], u_s], axis=1))
        else:
            w_in = bf(gmlp_w_in[li])
            w_outs = [bf(gmlp_w_out[li])]
            lg, lb = vec(gmlp_ln_g[li]), vec(gmlp_ln_b[li])
            dg = GMLP_W // GMLP_GROUPS
            bias = jnp.repeat(gmlp_b_s[li].T, dg, axis=1)
            acts_p = [_gmlp_in(hp, gmix, w_in, lg, lb, seq_mode=True, w_s=gmlp_w_s[li], bias=bias)]
            wrow = vec(jnp.repeat(gmlp_w_s[li][:, 0, 0], dg))
            brow = vec(jnp.repeat(gmlp_b_s[li][:, 0], dg))
            act_s, gv = _gmlp_in(hs, gmix, w_in, lg, lb, seq_mode=False, wrow=wrow, brow=brow)
            acts_s = [act_s]
            gv_s.append(gv.reshape(b_s, t_s, GMLP_W))

        gcross, w_q, g_q = vec(norm_cross[layer]), bf(cross_w_q[layer]), vec(cross_g_q[layer])
        w_o = bf(cross_w_o[layer])
        mk, mv, mkb, mvb = _mem_kv(mem_prompt.reshape(-1, D_MODEL), vec(norm_mem[layer]), bf(cross_w_k[layer]),
                                   bf(cross_w_v[layer]), vec(cross_g_k[layer]))
        mk_p.append(mk.reshape(b_p, -1, MEM_HEADS, MEM_HEAD_DIM))
        mv_p.append(mv.reshape(b_p, -1, MEM_HEADS, MEM_HEAD_DIM))
        gffn, w1, w2 = vec(norm_ffn[layer]), bf(ffn_w1[layer]), bf(ffn_w2[layer])

        hp = _post_mix(hp, acts_p, w_outs, gcross, w_q, g_q, mem=(mkb, mvb), w_o=w_o)
        hp = _ffn(hp, gffn, w1, w2)

        h1_s, qn_s = _post_mix(hs, acts_s, w_outs, gcross, w_q, g_q)
        stored = lambda c: _mem_chunks(c).reshape((-1,) + c.shape[2:3] + (D_MODEL // V7X_LANES, V7X_LANES))
        o_s = _mem_attend_rows(qn_s, stored(cache_mem_k), stored(cache_mem_v), layer * b_s)
        hs = _ffn(h1_s, gffn, w1, w2, pre=(o_s, w_o))

    return (hp.reshape(b_p, t_p, D_MODEL), hs.reshape(b_s, t_s, D_MODEL), jnp.stack(ak_p), jnp.stack(av_p),
            jnp.stack(ak_s), jnp.stack(av_s), jnp.stack(cs_p), jnp.stack(cs_s), jnp.stack(gv_s),
            jnp.stack(mk_p), jnp.stack(mv_p))
```

```python
import functools
import math

import jax
import jax.numpy as jnp
from jax import lax
from jax.experimental import pallas as pl
from jax.experimental.pallas import tpu as pltpu

D_MODEL = 1024
ATTN_HEADS = 8
HEAD_DIM = 64
ATTN_W = ATTN_HEADS * HEAD_DIM
ROT_DIM = HEAD_DIM // 4
ROPE_THETA = 500000.0
MOBA_BLOCK = 256
MOBA_TOPK = 3
PAGE_SIZE = 128
CONV_W = D_MODEL - ATTN_W
CONV_WIDTH = 3
GMLP_W = D_MODEL
GMLP_GROUPS = 4
GMLP_CHUNK = 128
MEM_HEADS = 4
MEM_HEAD_DIM = D_MODEL // MEM_HEADS
FFN_W = 4 * D_MODEL
EPS = 1e-6

MXU_DTYPE = jnp.bfloat16
V7X_LANES = 128
V7X_VMEM_BYTES = 64 * 1024 * 1024
VMEM_LIMIT = V7X_VMEM_BYTES * 7 // 8
NEG = -1e30
F32 = jnp.float32

ROW_TILE = 1024
FFN_ROW_TILE = 1024
FFN_COL_TILE = 1024
PAGES_PER_BLOCK = MOBA_BLOCK // PAGE_SIZE
SCORE_PAGES_PER_STEP = 16
SCORE_RING = 3
V_AUG_ROWS = HEAD_DIM + 16
NT_DIMS = (((1,), (1,)), ((), ()))
LOG2_SCORE_SCALE = HEAD_DIM ** -0.5 * math.log2(math.e)


def _cparams(*semantics):
    return pltpu.CompilerParams(dimension_semantics=semantics, vmem_limit_bytes=VMEM_LIMIT)


def _full(shape):
    n = len(shape)
    return pl.BlockSpec(shape, lambda *_: (0,) * n)


def _rows(tm, width):
    return pl.BlockSpec((tm, width), lambda i: (i, 0))


def _rms(x, g):
    return x * lax.rsqrt(jnp.mean(x * x, axis=-1, keepdims=True) + EPS) * g


def _mm(a, b):
    return jnp.dot(a, b, preferred_element_type=F32)


def _top_ids(g, lanef):
    ids = []
    for _ in range(MOBA_TOPK):
        mx = jnp.max(g, axis=1, keepdims=True)
        idx = jnp.min(jnp.where(g == mx, lanef, float(V7X_LANES)), axis=1, keepdims=True)
        ids.append(jnp.where(mx > 0.5 * NEG, idx, -1.0))
        g = jnp.where(lanef == idx, NEG, g)
    return ids


def _head_rms_rows(t, g, hm):
    t2 = t * t
    hi = t2.astype(MXU_DTYPE)
    lo = (t2 - hi.astype(F32)).astype(MXU_DTYPE)
    ss = _mm(hi, hm) + _mm(lo, hm)
    return t * lax.rsqrt(ss * (1.0 / HEAD_DIM) + EPS) * g


def _rope_rows(t, pos, inv_lanes):
    ang = pos * inv_lanes
    cs, sn = jnp.cos(ang), jnp.sin(ang)
    lane = lax.broadcasted_iota(jnp.int32, ang.shape, 1) & (HEAD_DIM - 1)
    half = ROT_DIM // 2
    reps = ATTN_W // V7X_LANES
    c_t = jnp.concatenate([jnp.where(lane < ROT_DIM, cs, 1.0)] * reps, axis=1)
    s_up = jnp.concatenate([jnp.where((lane >= half) & (lane < ROT_DIM), sn, 0.0)] * reps, axis=1)
    s_dn = jnp.concatenate([jnp.where(lane < half, -sn, 0.0)] * reps, axis=1)
    return t * c_t + pltpu.roll(t, half, 1) * s_up + pltpu.roll(t, ATTN_W - half, 1) * s_dn


def _conv_taps(u2, u1, u, cw):
    return u2 * cw[0:1, :] + u1 * cw[1:2, :] + u * cw[2:3, :]


def _mix_in_seq_body(tm, x_ref, g_ref, w_ref, wt_ref, gqc_ref, gkc_ref, inv8_ref, cw_ref,
                     qt_ref, kt_ref, kb_ref, vt_ref, vtb_ref, cm_ref, kmean_ref, tail_ref, ubuf):
    i = pl.program_id(0)
    xn = _rms(x_ref[...], g_ref[...]).astype(MXU_DTYPE)
    zt = lax.dot_general(wt_ref[...], xn, NT_DIMS, preferred_element_type=F32)
    post = (i * tm + lax.broadcasted_iota(jnp.int32, (1, tm), 1)).astype(F32)
    ang = inv8_ref[...] * post
    cs, sn = jnp.cos(ang), jnp.sin(ang)
    half = ROT_DIM // 2

    def norm_rope(z, gain_ref):
        pieces = []
        for h in range(ATTN_HEADS):
            t = z[h * HEAD_DIM:(h + 1) * HEAD_DIM]
            t = t * lax.rsqrt(jnp.mean(t * t, axis=0, keepdims=True) + EPS) * gain_ref[h * HEAD_DIM:(h + 1) * HEAD_DIM]
            x1, x2 = t[0:half], t[half:ROT_DIM]
            pieces += [x1 * cs - x2 * sn, x2 * cs + x1 * sn, t[ROT_DIM:]]
        return jnp.concatenate(pieces, axis=0)

    qt = (norm_rope(zt[:ATTN_W], gqc_ref) * LOG2_SCORE_SCALE).astype(qt_ref.dtype)
    for b in range(tm // MOBA_BLOCK):
        qt_ref[b] = qt[:, b * MOBA_BLOCK:(b + 1) * MOBA_BLOCK]
    kt = norm_rope(zt[ATTN_W:2 * ATTN_W], gkc_ref)
    kt_ref[...] = kt
    k = kt.T
    kb_ref[...] = k.astype(kb_ref.dtype)
    for b in range(tm // MOBA_BLOCK):
        kmean_ref[0, b:b + 1, :] = jnp.mean(k[b * MOBA_BLOCK:(b + 1) * MOBA_BLOCK], axis=0, keepdims=True)
    vt = zt[2 * ATTN_W:]
    vt_ref[...] = vt
    ones = jnp.ones((V_AUG_ROWS - HEAD_DIM, MOBA_BLOCK), vtb_ref.dtype)
    for b in range(tm // MOBA_BLOCK):
        for h in range(ATTN_HEADS):
            vtb_ref[b, h, :HEAD_DIM, :] = vt[h * HEAD_DIM:(h + 1) * HEAD_DIM,
                                             b * MOBA_BLOCK:(b + 1) * MOBA_BLOCK].astype(vtb_ref.dtype)
            vtb_ref[b, h, HEAD_DIM:, :] = ones

    def proj(c):
        return _mm(xn, w_ref[:, c * CONV_W:(c + 1) * CONV_W])

    bg = proj(0)
    u = proj(1) * proj(2)

    @pl.when(i == 0)
    def _():
        ubuf[0:8, :] = jnp.zeros((8, CONV_W), F32)

    ubuf[8:8 + tm, :] = u
    conv = _conv_taps(ubuf[6:6 + tm, :], ubuf[7:7 + tm, :], u, cw_ref[...])
    ubuf[0:8, :] = u[tm - 8:tm, :]
    tail_ref[...] = u[tm - 8:tm, :]
    cm_ref[...] = (bg * conv).astype(cm_ref.dtype)


def _mix_in_seq(x, gain, w_rows, w_t, gq_col, gk_col, inv8, conv_w):
    t = x.shape[0]
    tm = min(ROW_TILE, t)
    assert t % tm == 0 and tm % MOBA_BLOCK == 0
    n, nb = t // tm, tm // MOBA_BLOCK
    f32t = jax.ShapeDtypeStruct((ATTN_W, t), F32)
    b16o = jax.ShapeDtypeStruct((t, ATTN_W), MXU_DTYPE)
    cols = pl.BlockSpec((ATTN_W, tm), lambda i: (0, i))
    return pl.pallas_call(
        functools.partial(_mix_in_seq_body, tm), grid=(n,),
        in_specs=[_rows(tm, D_MODEL), _full((1, D_MODEL)), _full(w_rows.shape), _full(w_t.shape),
                  _full((ATTN_W, 1)), _full((ATTN_W, 1)), _full((ROT_DIM // 2, 1)), _full((CONV_WIDTH, CONV_W))],
        out_specs=[pl.BlockSpec((nb, ATTN_W, MOBA_BLOCK), lambda i: (i, 0, 0)), cols, _rows(tm, ATTN_W), cols,
                   pl.BlockSpec((nb, ATTN_HEADS, V_AUG_ROWS, MOBA_BLOCK), lambda i: (i, 0, 0, 0)), _rows(tm, CONV_W),
                   pl.BlockSpec((1, nb, ATTN_W), lambda i: (i, 0, 0)), _full((8, CONV_W))],
        out_shape=[jax.ShapeDtypeStruct((t // MOBA_BLOCK, ATTN_W, MOBA_BLOCK), MXU_DTYPE), f32t, b16o, f32t,
                   jax.ShapeDtypeStruct((t // MOBA_BLOCK, ATTN_HEADS, V_AUG_ROWS, MOBA_BLOCK), MXU_DTYPE), b16o,
                   jax.ShapeDtypeStruct((n, nb, ATTN_W), F32), jax.ShapeDtypeStruct((8, CONV_W), F32)],
        scratch_shapes=[pltpu.VMEM((tm + 8, CONV_W), F32)],
        compiler_params=_cparams("arbitrary"), name="mix_in_seq",
    )(x, gain, w_rows, w_t, gq_col, gk_col, inv8, conv_w)


def _mix_in_rows_body(pos, x_ref, g_ref, w_ref, gq_ref, gk_ref, hm_ref, inv_ref, cw_ref, p2_ref, p1_ref,
                      qs_ref, k_ref, v_ref, cm_ref, u_ref):
    xn = _rms(x_ref[...], g_ref[...]).astype(MXU_DTYPE)

    def proj(c):
        return _mm(xn, w_ref[:, c * ATTN_W:(c + 1) * ATTN_W])

    posf = jnp.full((x_ref.shape[0], V7X_LANES), pos, F32)
    q = _rope_rows(_head_rms_rows(proj(0), gq_ref[...], hm_ref[...]), posf, inv_ref[...])
    k = _rope_rows(_head_rms_rows(proj(1), gk_ref[...], hm_ref[...]), posf, inv_ref[...])
    qs_ref[...] = q * (HEAD_DIM ** -0.5)
    k_ref[...] = k
    v_ref[...] = proj(2)
    bg = proj(3)
    u = proj(4) * proj(5)
    u_ref[...] = u
    cm_ref[...] = (bg * _conv_taps(p2_ref[...], p1_ref[...], u, cw_ref[...])).astype(cm_ref.dtype)


def _mix_in_rows(x, gain, w_in, gq, gk, hmat, inv_lanes, conv_w, prev2, prev1, *, pos):
    t = x.shape[0]
    f32o = jax.ShapeDtypeStruct((t, ATTN_W), F32)
    blk = _rows(t, ATTN_W)
    return pl.pallas_call(
        functools.partial(_mix_in_rows_body, pos), grid=(1,),
        in_specs=[_rows(t, D_MODEL), _full((1, D_MODEL)), _full(w_in.shape), _full((1, ATTN_W)), _full((1, ATTN_W)),
                  _full(hmat.shape), _full((1, V7X_LANES)), _full((CONV_WIDTH, CONV_W)), blk, blk],
        out_specs=[blk] * 5,
        out_shape=[f32o, f32o, f32o, jax.ShapeDtypeStruct((t, ATTN_W), MXU_DTYPE), f32o],
        compiler_params=_cparams("arbitrary"), name="mix_in_rows",
    )(x, gain, w_in, gq, gk, hmat, inv_lanes, conv_w, prev2, prev1)


def _moba_seq_body(qt_ref, k_ref, vt_ref, km_ref, o_ref, m_scr, acc_scr, id_scr, s_even, s_odd):
    def one_block(i, carry):
        rows = pl.ds(pl.multiple_of(i * MOBA_BLOCK, MOBA_BLOCK), MOBA_BLOCK)
        _moba_query_block(i, qt_ref[i], k_ref, vt_ref, km_ref, o_ref.at[rows], m_scr, acc_scr, id_scr, s_even, s_odd)
        return carry

    lax.fori_loop(0, qt_ref.shape[0], one_block, 0)


def _moba_query_block(i, qt, k_ref, vt_ref, km_ref, o_ref, m_scr, acc_scr, id_scr, s_even, s_odd):
    blk = MOBA_BLOCK
    last = vt_ref.shape[0] - 1
    frow = lax.broadcasted_iota(jnp.int32, (V7X_LANES, blk), 0)
    qh = [jnp.where((frow < HEAD_DIM) == (h == 0), qt, jnp.zeros_like(qt)) for h in range(2)]
    brow = frow.astype(F32)
    krow = lax.broadcasted_iota(jnp.int32, (blk, blk), 0)
    qcol = lax.broadcasted_iota(jnp.int32, (blk, blk), 1)

    def keys(j):
        return k_ref[pl.ds(pl.multiple_of(j * blk, blk), blk), :]

    def raw_scores(dst, step):
        for d in range(2):
            kd = keys(jnp.minimum(2 * step + d, last))
            for h in range(2):
                dst[h, d] = _mm(kd, qh[h])

    def consume(src, step):
        j0 = 2 * step
        for h in range(2):
            vs = jnp.concatenate([vt_ref[jnp.minimum(j0 + d, last), h] for d in range(2)], axis=1)
            ss, picked = [], []
            for d in range(2):
                jf = jnp.asarray(j0 + d, F32)
                picked.append((id_scr[h, 0:1, :] == jf) | (id_scr[h, 1:2, :] == jf) | (id_scr[h, 2:3, :] == jf))
                ss.append(src[h, d])
            m_prev = m_scr[h][0:1]
            m_new = m_prev
            for d in range(2):
                m_new = jnp.maximum(m_new, jnp.where(picked[d], jnp.max(ss[d], axis=0, keepdims=True), NEG))
            p = jnp.concatenate([jnp.exp2(ss[d] - jnp.where(picked[d], m_new, -NEG)) for d in range(2)],
                                axis=0).astype(MXU_DTYPE)
            acc_scr[h] = jnp.exp2(m_prev - m_new) * acc_scr[h] + _mm(vs, p)
            m_scr[h] = jnp.broadcast_to(m_new, (8, blk))

    k_own = keys(i)
    for h in range(2):
        gate = jnp.where(frow < i, _mm(km_ref[...], qh[h]), NEG)
        for r in range(MOBA_TOPK):
            mx = jnp.max(gate, axis=0, keepdims=True)
            idx = jnp.min(jnp.where(gate == mx, brow, float(V7X_LANES)), axis=0, keepdims=True)
            id_scr[h, r:r + 1, :] = jnp.where(mx > 0.5 * NEG, idx, -1.0)
            gate = jnp.where(brow == idx, NEG, gate)
        s = jnp.where(krow <= qcol, _mm(k_own, qh[h]), NEG)
        m = jnp.max(s, axis=0, keepdims=True)
        m_scr[h] = jnp.broadcast_to(m, (8, blk))
        acc_scr[h] = _mm(vt_ref[i, h], jnp.exp2(s - m).astype(MXU_DTYPE))

    raw_scores(s_even, 0)

    def two_steps(t, carry):
        raw_scores(s_odd, 2 * t + 1)
        consume(s_even, 2 * t)
        raw_scores(s_even, 2 * t + 2)
        consume(s_odd, 2 * t + 1)
        return carry

    lax.fori_loop(0, (i + 3) // 4, two_steps, 0)
    ot = jnp.concatenate([acc_scr[h][:HEAD_DIM] / acc_scr[h][HEAD_DIM:HEAD_DIM + 1] for h in range(2)], axis=0)
    o_ref[...] = ot.T.astype(o_ref.dtype)


def _moba_seq(qt, kb, vtb, kmean_pad):
    t = kb.shape[0]
    nblk = t // MOBA_BLOCK
    assert t % MOBA_BLOCK == 0 and nblk <= V7X_LANES
    pairs = ATTN_W // V7X_LANES
    return pl.pallas_call(
        _moba_seq_body,
        grid=(pairs,),
        in_specs=[pl.BlockSpec((nblk, V7X_LANES, MOBA_BLOCK), lambda p: (0, p, 0)),
                  pl.BlockSpec((t, V7X_LANES), lambda p: (0, p)),
                  pl.BlockSpec((nblk, 2, V_AUG_ROWS, MOBA_BLOCK), lambda p: (0, p, 0, 0)),
                  pl.BlockSpec((V7X_LANES, V7X_LANES), lambda p: (0, p))],
        out_specs=pl.BlockSpec((t, V7X_LANES), lambda p: (0, p)),
        out_shape=jax.ShapeDtypeStruct((t, ATTN_W), MXU_DTYPE),
        scratch_shapes=[pltpu.VMEM((2, 8, MOBA_BLOCK), F32), pltpu.VMEM((2, V_AUG_ROWS, MOBA_BLOCK), F32),
                        pltpu.VMEM((2, 8, MOBA_BLOCK), F32)] + [pltpu.VMEM((2, 2, MOBA_BLOCK, MOBA_BLOCK), F32)] * 2,
        compiler_params=_cparams("arbitrary"), name="moba_seq")(qt, kb, vtb, kmean_pad)


def _head_rows(q_row):
    sub = lax.broadcasted_iota(jnp.int32, (ATTN_HEADS, ATTN_W), 0)
    lane = lax.broadcasted_iota(jnp.int32, (ATTN_HEADS, ATTN_W), 1)
    return jnp.where(lane // HEAD_DIM == sub, jnp.broadcast_to(q_row, (ATTN_HEADS, ATTN_W)), 0.0)


def _moba_paged_scores_body(n_pages, pt_ref, q_ref, kn_ref, k_hbm, p_ref, ids_ref, pn_ref, kbuf, sem, gsum):
    per = SCORE_PAGES_PER_STEP
    groups = n_pages // per
    t = pl.program_id(0)
    n_steps = pl.num_programs(0)
    grp = t % groups
    slot = t % SCORE_RING

    def page_copy(step, g):
        buf = step % SCORE_RING
        return pltpu.make_async_copy(k_hbm.at[pt_ref[step * per + g]], kbuf.at[buf, g], sem.at[buf])

    def start_step(step):
        @pl.when(step < n_steps)
        def _():
            for g in range(per):
                page_copy(step, g).start()

    @pl.when(t == 0)
    def _():
        for ahead in range(SCORE_RING - 1):
            start_step(t + ahead)

    start_step(t + SCORE_RING - 1)
    for g in range(per):
        page_copy(t, g).wait()

    qd = _head_rows(q_ref[0])
    qb = qd.astype(MXU_DTYPE)
    lane = lax.broadcasted_iota(jnp.int32, (ATTN_HEADS, V7X_LANES), 1)

    @pl.when(grp == 0)
    def _():
        gsum[...] = jnp.zeros_like(gsum)

    for g in range(per):
        pg = grp * per + g
        s = _mm(qb, kbuf[slot, g].astype(MXU_DTYPE))
        p_ref[0, pg] = s
        gsum[...] += jnp.where(lane == pg // PAGES_PER_BLOCK, jnp.sum(s, axis=1, keepdims=True), 0.0)

    @pl.when(grp == groups - 1)
    def _():
        n_blocks = n_pages // PAGES_PER_BLOCK
        lanef = lane.astype(F32)
        gate = jnp.where(lane < n_blocks, gsum[...] * (1.0 / MOBA_BLOCK), NEG)
        ids = _top_ids(gate, lanef)
        for r in range(MOBA_TOPK):
            ids_ref[0, r] = jnp.broadcast_to(ids[r], (ATTN_HEADS, V7X_LANES)).astype(jnp.int32)
        s_new = jnp.sum(qd * kn_ref[0], axis=1, keepdims=True)
        sc = p_ref[0]
        blk = (lax.broadcasted_iota(jnp.int32, sc.shape, 0) // PAGES_PER_BLOCK).astype(F32)
        picked = (blk == ids[0][None]) | (blk == ids[1][None]) | (blk == ids[2][None])
        sc = jnp.where(picked, sc, NEG)
        m = jnp.maximum(jnp.max(jnp.max(sc, axis=0), axis=1, keepdims=True), s_new)
        e = jnp.where(picked, jnp.exp(sc - m[None]), 0.0)
        e_new = jnp.exp(s_new - m)
        inv = 1.0 / (jnp.sum(jnp.sum(e, axis=0), axis=1, keepdims=True) + e_new)
        p_ref[0] = e * inv[None]
        pn_ref[0] = jnp.broadcast_to(e_new * inv, (ATTN_HEADS, V7X_LANES))


def _moba_paged_pv_body(n_pages, pt_ref, ids_ref, p_ref, pn_ref, vn_ref, v_hbm, o_ref, vbuf, sem):
    b = pl.program_id(0)
    slot = b % 2
    n_slots = ATTN_HEADS * MOBA_TOPK

    def first_page(seq, s):
        return jnp.maximum(ids_ref[seq * n_slots + s], 0) * PAGES_PER_BLOCK

    def slab_copy(seq, buf, h, r, half):
        page = pt_ref[seq * n_pages + first_page(seq, h * MOBA_TOPK + r) + half]
        rows = pl.ds(h * HEAD_DIM, HEAD_DIM)
        return pltpu.make_async_copy(v_hbm.at[page, rows], vbuf.at[buf, r * PAGES_PER_BLOCK + half, rows],
                                     sem.at[buf])

    def for_all_slabs(fn):
        for h in range(ATTN_HEADS):
            for r in range(MOBA_TOPK):
                for half in range(PAGES_PER_BLOCK):
                    fn(h, r, half)

    @pl.when(b == 0)
    def _():
        for_all_slabs(lambda h, r, half: slab_copy(b, slot, h, r, half).start())

    @pl.when(b + 1 < pl.num_programs(0))
    def _():
        for_all_slabs(lambda h, r, half: slab_copy(b + 1, 1 - slot, h, r, half).start())

    for_all_slabs(lambda h, r, half: slab_copy(b, slot, h, r, half).wait())

    sub = lax.broadcasted_iota(jnp.int32, (ATTN_HEADS, V7X_LANES), 0)
    full = pn_ref[0][:, 0:1] * vn_ref[0]
    for r in range(MOBA_TOPK):
        for half in range(PAGES_PER_BLOCK):
            w = jnp.zeros((ATTN_HEADS, V7X_LANES), F32)
            for h in range(ATTN_HEADS):
                w = jnp.where(sub == h, p_ref[0, first_page(b, h * MOBA_TOPK + r) + half], w)
            full = full + lax.dot_general(w.astype(MXU_DTYPE),
                                          vbuf[slot, r * PAGES_PER_BLOCK + half].astype(MXU_DTYPE), NT_DIMS,
                                          preferred_element_type=F32)
    sub_w = lax.broadcasted_iota(jnp.int32, (ATTN_HEADS, ATTN_W), 0)
    lane_w = lax.broadcasted_iota(jnp.int32, (ATTN_HEADS, ATTN_W), 1)
    o_ref[0] = jnp.sum(jnp.where(lane_w // HEAD_DIM == sub_w, full, 0.0), axis=0, keepdims=True).astype(o_ref.dtype)


def _moba_paged(qs, k_new, v_new, cache_k, cache_v, page_ids):
    nb, n_pages = page_ids.shape
    per = SCORE_PAGES_PER_STEP
    assert n_pages % PAGES_PER_BLOCK == 0 and n_pages // PAGES_PER_BLOCK <= V7X_LANES and n_pages % per == 0
    groups = n_pages // per
    pt = page_ids.reshape(-1)
    row3 = lambda a: a.reshape(nb, 1, ATTN_W)
    page_shape = (ATTN_W, PAGE_SIZE)

    def per_seq(steps_per_seq, *shape):
        return pl.BlockSpec((1,) + shape, lambda t, *_: (t // steps_per_seq,) + (0,) * len(shape))

    probs, ids, p_new = pl.pallas_call(
        functools.partial(_moba_paged_scores_body, n_pages),
        grid_spec=pltpu.PrefetchScalarGridSpec(
            num_scalar_prefetch=1, grid=(nb * groups,),
            in_specs=[per_seq(groups, 1, ATTN_W), per_seq(groups, 1, ATTN_W), pl.BlockSpec(memory_space=pl.ANY)],
            out_specs=[per_seq(groups, n_pages, ATTN_HEADS, V7X_LANES),
                       per_seq(groups, MOBA_TOPK, ATTN_HEADS, V7X_LANES), per_seq(groups, ATTN_HEADS, V7X_LANES)],
            scratch_shapes=[pltpu.VMEM((SCORE_RING, per) + page_shape, F32), pltpu.SemaphoreType.DMA((SCORE_RING,)),
                            pltpu.VMEM((ATTN_HEADS, V7X_LANES), F32)]),
        out_shape=[jax.ShapeDtypeStruct((nb, n_pages, ATTN_HEADS, V7X_LANES), F32),
                   jax.ShapeDtypeStruct((nb, MOBA_TOPK, ATTN_HEADS, V7X_LANES), jnp.int32),
                   jax.ShapeDtypeStruct((nb, ATTN_HEADS, V7X_LANES), F32)],
        compiler_params=_cparams("arbitrary"), name="moba_paged_scores",
    )(pt, row3(qs), row3(k_new), cache_k)
    slot_ids = jnp.transpose(ids[:, :, :, 0], (0, 2, 1)).reshape(-1)
    return pl.pallas_call(
        functools.partial(_moba_paged_pv_body, n_pages),
        grid_spec=pltpu.PrefetchScalarGridSpec(
            num_scalar_prefetch=2, grid=(nb,),
            in_specs=[per_seq(1, n_pages, ATTN_HEADS, V7X_LANES), per_seq(1, ATTN_HEADS, V7X_LANES),
                      per_seq(1, 1, ATTN_W), pl.BlockSpec(memory_space=pl.ANY)],
            out_specs=per_seq(1, 1, ATTN_W),
            scratch_shapes=[pltpu.VMEM((2, MOBA_TOPK * PAGES_PER_BLOCK) + page_shape, F32),
                            pltpu.SemaphoreType.DMA((2,))]),
        out_shape=jax.ShapeDtypeStruct((nb, 1, ATTN_W), MXU_DTYPE),
        compiler_params=_cparams("arbitrary"), name="moba_paged_pv",
    )(pt, slot_ids, probs, p_new, row3(v_new), cache_v).reshape(nb, ATTN_W)


def _gmlp_in_body(seq_mode, tm, *refs):
    if seq_mode:
        x_ref, g_ref, w_ref, lg_ref, lb_ref, ws_ref, bias_ref, act_ref = refs
    else:
        x_ref, g_ref, w_ref, lg_ref, lb_ref, wrow_ref, brow_ref, act_ref, v_ref = refs
    xn = _rms(x_ref[...], g_ref[...]).astype(MXU_DTYPE)
    u = jax.nn.gelu(_mm(xn, w_ref[:, :GMLP_W]))
    v = jax.nn.gelu(_mm(xn, w_ref[:, GMLP_W:]))
    vc = v - jnp.mean(v, axis=-1, keepdims=True)
    v = vc * lax.rsqrt(jnp.mean(vc * vc, axis=-1, keepdims=True) + EPS) * lg_ref[...] + lb_ref[...]
    if not seq_mode:
        v_ref[...] = v
        act_ref[...] = (u * (v * wrow_ref[...] + brow_ref[...])).astype(act_ref.dtype)
        return
    dg = GMLP_W // GMLP_GROUPS
    r = lax.broadcasted_iota(jnp.int32, (GMLP_CHUNK, GMLP_CHUNK), 0)
    c = lax.broadcasted_iota(jnp.int32, (GMLP_CHUNK, GMLP_CHUNK), 1)
    vb = v.astype(MXU_DTYPE)
    for g in range(GMLP_GROUPS):
        ws = jnp.where(c <= r, ws_ref[g], 0.0).astype(MXU_DTYPE)
        cols = slice(g * dg, (g + 1) * dg)
        for ch in range(tm // GMLP_CHUNK):
            rows = slice(ch * GMLP_CHUNK, (ch + 1) * GMLP_CHUNK)
            s = _mm(ws, vb[rows, cols]) + bias_ref[:, cols]
            act_ref[rows, cols] = (u[rows, cols] * s).astype(act_ref.dtype)


def _gmlp_in(x, gain, w_in, ln_g, ln_b, *, seq_mode, w_s=None, bias=None, wrow=None, brow=None):
    t = x.shape[0]
    tm = min(ROW_TILE, t)
    assert t % tm == 0 and (not seq_mode or tm % GMLP_CHUNK == 0)
    vec = _full((1, GMLP_W))
    in_specs = [_rows(tm, D_MODEL), vec, _full(w_in.shape), vec, vec]
    act = jax.ShapeDtypeStruct((t, GMLP_W), MXU_DTYPE)
    if seq_mode:
        in_specs += [_full(w_s.shape), _full(bias.shape)]
        args = [x, gain, w_in, ln_g, ln_b, w_s, bias]
        out_shape, out_specs = act, _rows(tm, GMLP_W)
    else:
        in_specs += [vec, vec]
        args = [x, gain, w_in, ln_g, ln_b, wrow, brow]
        out_shape = [act, jax.ShapeDtypeStruct((t, GMLP_W), F32)]
        out_specs = [_rows(tm, GMLP_W)] * 2
    return pl.pallas_call(
        functools.partial(_gmlp_in_body, seq_mode, tm), grid=(t // tm,), in_specs=in_specs,
        out_specs=out_specs, out_shape=out_shape, compiler_params=_cparams("arbitrary"),
        name="gmlp_in_seq" if seq_mode else "gmlp_in_rows")(*args)


def _mem_kv_body(m_ref, g_ref, wk_ref, wv_ref, gk_ref, k_ref, v_ref, kb_ref, vb_ref):
    mm = _rms(m_ref[...], g_ref[...]).astype(MXU_DTYPE)
    k = _mm(mm, wk_ref[...])
    v = _mm(mm, wv_ref[...])
    k = jnp.concatenate(
        [_rms(k[:, h * MEM_HEAD_DIM:(h + 1) * MEM_HEAD_DIM], gk_ref[...]) for h in range(MEM_HEADS)], axis=1)
    k_ref[...] = k
    v_ref[...] = v
    kb_ref[...] = k.astype(kb_ref.dtype)
    vb_ref[...] = v.astype(vb_ref.dtype)


def _mem_kv(mem, gain, w_k, w_v, g_k):
    m = mem.shape[0]
    f32o = jax.ShapeDtypeStruct((m, D_MODEL), F32)
    b16o = jax.ShapeDtypeStruct((m, D_MODEL), MXU_DTYPE)
    blk = _full((m, D_MODEL))
    return pl.pallas_call(
        _mem_kv_body, grid=(1,),
        in_specs=[blk, _full((1, D_MODEL)), _full(w_k.shape), _full(w_v.shape), _full((1, MEM_HEAD_DIM))],
        out_specs=[blk] * 4, out_shape=[f32o, f32o, b16o, b16o],
        compiler_params=_cparams("arbitrary"), name="mem_kv")(mem, gain, w_k, w_v, g_k)


def _post_mix_body(n_act, shared_mem, *refs):
    h_ref = refs[0]
    act_refs = refs[1:1 + n_act]
    w_refs = refs[1 + n_act:1 + 2 * n_act]
    rest = refs[1 + 2 * n_act:]
    if shared_mem:
        g_ref, wq_ref, gq_ref, mk_ref, mv_ref, wo_ref, o_ref = rest
    else:
        g_ref, wq_ref, gq_ref, h1_ref, q_ref = rest
    h1 = h_ref[...]
    for a_ref, w_ref in zip(act_refs, w_refs):
        h1 = h1 + _mm(a_ref[...], w_ref[...])
    qc = _mm(_rms(h1, g_ref[...]).astype(MXU_DTYPE), wq_ref[...])
    heads = []
    for hd in range(MEM_HEADS):
        cols = slice(hd * MEM_HEAD_DIM, (hd + 1) * MEM_HEAD_DIM)
        qn = _rms(qc[:, cols], gq_ref[...])
        if not shared_mem:
            heads.append(qn)
            continue
        s = lax.dot_general(qn.astype(MXU_DTYPE), mk_ref[:, cols], NT_DIMS,
                            preferred_element_type=F32) * (MEM_HEAD_DIM ** -0.5)
        p = jnp.exp(s - jnp.max(s, axis=1, keepdims=True))
        o = _mm(p.astype(MXU_DTYPE), mv_ref[:, cols]) / jnp.sum(p, axis=1, keepdims=True)
        heads.append(o.astype(MXU_DTYPE))
    cat = jnp.concatenate(heads, axis=1)
    if shared_mem:
        o_ref[...] = h1 + _mm(cat, wo_ref[...])
    else:
        h1_ref[...] = h1
        q_ref[...] = cat


def _post_mix(h, acts, w_outs, gain, w_q, g_q, mem=None, w_o=None):
    t = h.shape[0]
    tm = min(ROW_TILE, t)
    assert t % tm == 0
    shared = mem is not None
    in_specs = [_rows(tm, D_MODEL)] + [_rows(tm, a.shape[1]) for a in acts] + [_full(w.shape) for w in w_outs]
    in_specs += [_full((1, D_MODEL)), _full(w_q.shape), _full((1, MEM_HEAD_DIM))]
    args = [h, *acts, *w_outs, gain, w_q, g_q]
    f32o = jax.ShapeDtypeStruct((t, D_MODEL), F32)
    if shared:
        in_specs += [_full(mem[0].shape), _full(mem[1].shape), _full(w_o.shape)]
        args += [mem[0], mem[1], w_o]
        out_shape, out_specs = f32o, _rows(tm, D_MODEL)
    else:
        out_shape, out_specs = [f32o, f32o], [_rows(tm, D_MODEL)] * 2
    return pl.pallas_call(
        functools.partial(_post_mix_body, len(acts), shared), grid=(t // tm,), in_specs=in_specs,
        out_specs=out_specs, out_shape=out_shape, compiler_params=_cparams("arbitrary"),
        name="post_mix_shared" if shared else "post_mix_rows")(*args)


def _mem_chunks(a):
    lead = a.shape[:-2]
    a = a.reshape(lead + (MEM_HEADS, MEM_HEAD_DIM // V7X_LANES, V7X_LANES))
    return jnp.swapaxes(a, -3, -2).reshape(lead + (MEM_HEADS * MEM_HEAD_DIM // V7X_LANES, V7X_LANES))


def _mem_unchunk(a):
    lead = a.shape[:-2]
    a = a.reshape(lead + (MEM_HEAD_DIM // V7X_LANES, MEM_HEADS, V7X_LANES))
    return jnp.swapaxes(a, -3, -2).reshape(lead + (D_MODEL,))


def _mem_attend_rows_body(q_ref, k_ref, v_ref, o_ref):
    t = k_ref[0] * q_ref[0][None]
    t = t + pltpu.roll(t, MEM_HEADS, 1)
    s = jnp.sum(t, axis=2, keepdims=True) * (MEM_HEAD_DIM ** -0.5)
    p = jnp.exp(s - jnp.max(s, axis=0, keepdims=True))
    o = jnp.sum(p * v_ref[0], axis=0) / jnp.sum(p, axis=0)
    o_ref[0] = o.astype(o_ref.dtype)


def _mem_attend_rows(qn, mem_k, mem_v, first):
    nb, m = qn.shape[0], mem_k.shape[1]
    chunks = mem_k.shape[2:]
    row = pl.BlockSpec((1,) + chunks, lambda b: (b, 0, 0))
    memb = pl.BlockSpec((1, m) + chunks, lambda b: (first + b, 0, 0, 0))
    q8 = _mem_chunks(qn.reshape(nb, MEM_HEADS, MEM_HEAD_DIM))
    o8 = pl.pallas_call(
        _mem_attend_rows_body, grid=(nb,), in_specs=[row, memb, memb], out_specs=row,
        out_shape=jax.ShapeDtypeStruct((nb,) + chunks, F32),
        compiler_params=_cparams("arbitrary"), name="mem_attend_rows")(q8, mem_k, mem_v)
    return _mem_unchunk(o8).astype(MXU_DTYPE)


def _ffn_body(pre_proj, *refs):
    if pre_proj:
        h_ref, o_in_ref, wo_ref, g_ref, w1_ref, w2_ref, out_ref, h_scr, xn_scr, acc = refs
    else:
        h_ref, g_ref, w1_ref, w2_ref, out_ref, h_scr, xn_scr, acc = refs
    f = pl.program_id(1)

    @pl.when(f == 0)
    def _():
        h = h_ref[...]
        if pre_proj:
            h = h + _mm(o_in_ref[...], wo_ref[...])
        h_scr[...] = h
        xn_scr[...] = _rms(h, g_ref[...]).astype(xn_scr.dtype)
        acc[...] = jnp.zeros_like(acc)

    a = jnp.maximum(_mm(xn_scr[...], w1_ref[...]), 0.0)
    acc[...] += _mm((a * a).astype(MXU_DTYPE), w2_ref[...])

    @pl.when(f == pl.num_programs(1) - 1)
    def _():
        out_ref[...] = h_scr[...] + acc[...]


def _ffn(h, gain, w1, w2, pre=None):
    t = h.shape[0]
    tm = min(FFN_ROW_TILE, t)
    tf = FFN_COL_TILE
    assert t % tm == 0 and FFN_W % tf == 0
    rows = pl.BlockSpec((tm, D_MODEL), lambda i, f: (i, 0))
    in_specs, args = [rows], [h]
    if pre is not None:
        in_specs += [rows, pl.BlockSpec(pre[1].shape, lambda i, f: (0, 0))]
        args += list(pre)
    in_specs += [pl.BlockSpec((1, D_MODEL), lambda i, f: (0, 0)),
                 pl.BlockSpec((D_MODEL, tf), lambda i, f: (0, f)), pl.BlockSpec((tf, D_MODEL), lambda i, f: (f, 0))]
    args += [gain, w1, w2]
    return pl.pallas_call(
        functools.partial(_ffn_body, pre is not None), grid=(t // tm, FFN_W // tf), in_specs=in_specs,
        out_specs=rows, out_shape=jax.ShapeDtypeStruct((t, D_MODEL), F32),
        scratch_shapes=[pltpu.VMEM((tm, D_MODEL), F32), pltpu.VMEM((tm, D_MODEL), MXU_DTYPE),
                        pltpu.VMEM((tm, D_MODEL), F32)],
        compiler_params=_cparams("arbitrary", "arbitrary"), name="ffn")(*args)


def kernel(x_prompt, x_sample, mem_prompt, cache_attn_k, cache_attn_v, state_conv, cache_mem_k, cache_mem_v,
           page_table, norm_mix, norm_cross, norm_mem, norm_ffn, mix_w_in, attn_g_q, attn_g_k, conv_w, mix_w_out,
           gmlp_w_in, gmlp_ln_g, gmlp_ln_b, gmlp_w_s, gmlp_b_s, gmlp_w_out, cross_w_q, cross_w_k, cross_w_v,
           cross_w_o, cross_g_q, cross_g_k, ffn_w1, ffn_w2):
    b_p, t_p, _ = x_prompt.shape
    b_s, t_s, _ = x_sample.shape
    assert b_p == 1 and t_s == 1
    depth = norm_mix.shape[0]
    n_pages = page_table.shape[1]
    past_len = n_pages * PAGE_SIZE
    bf = lambda a: a.astype(MXU_DTYPE)
    vec = lambda a: a.reshape(1, -1)

    half = ROT_DIM // 2
    inv = jnp.power(jnp.float32(ROPE_THETA), -jnp.arange(half, dtype=jnp.float32) * (2.0 / ROT_DIM))
    lane = jnp.arange(V7X_LANES) % HEAD_DIM
    inv_lanes = jnp.where(lane < ROT_DIM, inv[lane % half], 0.0).reshape(1, V7X_LANES).astype(F32)
    inv8 = inv.reshape(half, 1)
    hmat = bf(jnp.kron(jnp.eye(ATTN_HEADS, dtype=F32), jnp.ones((HEAD_DIM, HEAD_DIM), F32)))

    hp = x_prompt.reshape(t_p, D_MODEL)
    hs = x_sample.reshape(b_s, D_MODEL)
    ak_p, av_p, ak_s, av_s, cs_p, cs_s, gv_s, mk_p, mv_p = [], [], [], [], [], [], [], [], []
    for layer in range(depth):
        li = layer // 2
        gmix = vec(norm_mix[layer])
        if layer % 2 == 0:
            w_in = bf(mix_w_in[li])
            gq = jnp.tile(attn_g_q[li], ATTN_HEADS)
            gk = jnp.tile(attn_g_k[li], ATTN_HEADS)
            w_out = bf(mix_w_out[li])
            w_outs = [w_out[:ATTN_W], w_out[ATTN_W:]]
            qt, kt, kb, vt, vtb, cmix, kmean, tail = _mix_in_seq(
                hp, gmix, w_in[:, 3 * ATTN_W:], w_in[:, :3 * ATTN_W].T, gq.reshape(ATTN_W, 1), gk.reshape(ATTN_W, 1),
                inv8, conv_w[li])
            kmean = kmean.reshape(-1, ATTN_W)
            kmean_pad = bf(jnp.pad(kmean, ((0, V7X_LANES - kmean.shape[0]), (0, 0))))
            acts_p = [_moba_seq(qt, kb, vtb, kmean_pad), cmix]
            as_cache = lambda a: jnp.transpose(a.reshape(ATTN_HEADS, HEAD_DIM, t_p), (2, 0, 1))[None]
            ak_p.append(as_cache(kt))
            av_p.append(as_cache(vt))
            cs_p.append(tail[8 - (CONV_WIDTH - 1):].reshape(b_p, CONV_WIDTH - 1, CONV_W))

            qs_s, k_s, v_s, cmix_s, u_s = _mix_in_rows(
                hs, gmix, w_in, vec(gq), vec(gk), hmat, inv_lanes, conv_w[li], state_conv[li, :, 0], state_conv[li, :, 1],
                pos=past_len)
            pool = cache_attn_k.shape[1]
            paged = lambda c: jnp.transpose(c, (0, 1, 3, 4, 2)).reshape(-1, ATTN_W, PAGE_SIZE)
            attn_s = _moba_paged(qs_s, k_s, v_s, paged(cache_attn_k), paged(cache_attn_v), page_table + li * pool)
            acts_s = [attn_s, cmix_s]
            ak_s.append(k_s.reshape(b_s, t_s, ATTN_HEADS, HEAD_DIM))
            av_s.append(v_s.reshape(b_s, t_s, ATTN_HEADS, HEAD_DIM))
            cs_s.append(jnp.stack([state_conv[li, :, 1], u_s], axis=1))
        else:
            w_in = bf(gmlp_w_in[li])
            w_outs = [bf(gmlp_w_out[li])]
            lg, lb = vec(gmlp_ln_g[li]), vec(gmlp_ln_b[li])
            dg = GMLP_W // GMLP_GROUPS
            bias = jnp.repeat(gmlp_b_s[li].T, dg, axis=1)
            acts_p = [_gmlp_in(hp, gmix, w_in, lg, lb, seq_mode=True, w_s=gmlp_w_s[li], bias=bias)]
            wrow = vec(jnp.repeat(gmlp_w_s[li][:, 0, 0], dg))
            brow = vec(jnp.repeat(gmlp_b_s[li][:, 0], dg))
            act_s, gv = _gmlp_in(hs, gmix, w_in, lg, lb, seq_mode=False, wrow=wrow, brow=brow)
            acts_s = [act_s]
            gv_s.append(gv.reshape(b_s, t_s, GMLP_W))

        gcross, w_q, g_q = vec(norm_cross[layer]), bf(cross_w_q[layer]), vec(cross_g_q[layer])
        w_o = bf(cross_w_o[layer])
        mk, mv, mkb, mvb = _mem_kv(mem_prompt.reshape(-1, D_MODEL), vec(norm_mem[layer]), bf(cross_w_k[layer]),
                                   bf(cross_w_v[layer]), vec(cross_g_k[layer]))
        mk_p.append(mk.reshape(b_p, -1, MEM_HEADS, MEM_HEAD_DIM))
        mv_p.append(mv.reshape(b_p, -1, MEM_HEADS, MEM_HEAD_DIM))
        gffn, w1, w2 = vec(norm_ffn[layer]), bf(ffn_w1[layer]), bf(ffn_w2[layer])

        hp = _post_mix(hp, acts_p, w_outs, gcross, w_q, g_q, mem=(mkb, mvb), w_o=w_o)
        hp = _ffn(hp, gffn, w1, w2)

        h1_s, qn_s = _post_mix(hs, acts_s, w_outs, gcross, w_q, g_q)
        stored = lambda c: _mem_chunks(c).reshape((-1,) + c.shape[2:3] + (D_MODEL // V7X_LANES, V7X_LANES))
        o_s = _mem_attend_rows(qn_s, stored(cache_mem_k), stored(cache_mem_v), layer * b_s)
        hs = _ffn(h1_s, gffn, w1, w2, pre=(o_s, w_o))

    return (hp.reshape(b_p, t_p, D_MODEL), hs.reshape(b_s, t_s, D_MODEL), jnp.stack(ak_p), jnp.stack(av_p),
            jnp.stack(ak_s), jnp.stack(av_s), jnp.stack(cs_p), jnp.stack(cs_s), jnp.stack(gv_s),
            jnp.stack(mk_p), jnp.stack(mv_p))
```

```python
import functools
import math

import jax
import jax.numpy as jnp
from jax import lax
from jax.experimental import pallas as pl
from jax.experimental.pallas import tpu as pltpu

D_MODEL = 1024
ATTN_HEADS = 8
HEAD_DIM = 64
ATTN_W = ATTN_HEADS * HEAD_DIM
ROT_DIM = HEAD_DIM // 4
ROPE_THETA = 500000.0
MOBA_BLOCK = 256
MOBA_TOPK = 3
PAGE_SIZE = 128
CONV_W = D_MODEL - ATTN_W
CONV_WIDTH = 3
GMLP_W = D_MODEL
GMLP_GROUPS = 4
GMLP_CHUNK = 128
MEM_HEADS = 4
MEM_HEAD_DIM = D_MODEL // MEM_HEADS
FFN_W = 4 * D_MODEL
EPS = 1e-6

MXU_DTYPE = jnp.bfloat16
V7X_LANES = 128
V7X_VMEM_BYTES = 64 * 1024 * 1024
VMEM_LIMIT = V7X_VMEM_BYTES * 7 // 8
NEG = -1e30
F32 = jnp.float32

ROW_TILE = 1024
FFN_ROW_TILE = 1024
FFN_COL_TILE = 1024
PAGES_PER_BLOCK = MOBA_BLOCK // PAGE_SIZE
SCORE_PAGES_PER_STEP = 16
MOBA_QUERY_BLOCKS_PER_STEP = 4
SCORE_RING = 3
V_AUG_ROWS = HEAD_DIM + 16
NT_DIMS = (((1,), (1,)), ((), ()))
LOG2_SCORE_SCALE = HEAD_DIM ** -0.5 * math.log2(math.e)


def _cparams(*semantics):
    return pltpu.CompilerParams(dimension_semantics=semantics, vmem_limit_bytes=VMEM_LIMIT)


def _full(shape):
    n = len(shape)
    return pl.BlockSpec(shape, lambda *_: (0,) * n)


def _rows(tm, width):
    return pl.BlockSpec((tm, width), lambda i: (i, 0))


def _rms(x, g):
    return x * lax.rsqrt(jnp.mean(x * x, axis=-1, keepdims=True) + EPS) * g


def _mm(a, b):
    return jnp.dot(a, b, preferred_element_type=F32)


def _top_ids(g, lanef):
    ids = []
    for _ in range(MOBA_TOPK):
        mx = jnp.max(g, axis=1, keepdims=True)
        idx = jnp.min(jnp.where(g == mx, lanef, float(V7X_LANES)), axis=1, keepdims=True)
        ids.append(jnp.where(mx > 0.5 * NEG, idx, -1.0))
        g = jnp.where(lanef == idx, NEG, g)
    return ids


def _head_rms_rows(t, g, hm):
    t2 = t * t
    hi = t2.astype(MXU_DTYPE)
    lo = (t2 - hi.astype(F32)).astype(MXU_DTYPE)
    ss = _mm(hi, hm) + _mm(lo, hm)
    return t * lax.rsqrt(ss * (1.0 / HEAD_DIM) + EPS) * g


def _rope_rows(t, pos, inv_lanes):
    ang = pos * inv_lanes
    cs, sn = jnp.cos(ang), jnp.sin(ang)
    lane = lax.broadcasted_iota(jnp.int32, ang.shape, 1) & (HEAD_DIM - 1)
    half = ROT_DIM // 2
    reps = ATTN_W // V7X_LANES
    c_t = jnp.concatenate([jnp.where(lane < ROT_DIM, cs, 1.0)] * reps, axis=1)
    s_up = jnp.concatenate([jnp.where((lane >= half) & (lane < ROT_DIM), sn, 0.0)] * reps, axis=1)
    s_dn = jnp.concatenate([jnp.where(lane < half, -sn, 0.0)] * reps, axis=1)
    return t * c_t + pltpu.roll(t, half, 1) * s_up + pltpu.roll(t, ATTN_W - half, 1) * s_dn


def _conv_taps(u2, u1, u, cw):
    return u2 * cw[0:1, :] + u1 * cw[1:2, :] + u * cw[2:3, :]


def _mix_in_seq_body(tm, x_ref, g_ref, w_ref, wt_ref, gqc_ref, gkc_ref, inv8_ref, cw_ref,
                     qt_ref, kt_ref, kb_ref, vt_ref, vtb_ref, cm_ref, kmean_ref, tail_ref, ubuf):
    i = pl.program_id(0)
    xn = _rms(x_ref[...], g_ref[...]).astype(MXU_DTYPE)
    zt = lax.dot_general(wt_ref[...], xn, NT_DIMS, preferred_element_type=F32)
    post = (i * tm + lax.broadcasted_iota(jnp.int32, (1, tm), 1)).astype(F32)
    ang = inv8_ref[...] * post
    cs, sn = jnp.cos(ang), jnp.sin(ang)
    half = ROT_DIM // 2

    def norm_rope(z, gain_ref):
        pieces = []
        for h in range(ATTN_HEADS):
            t = z[h * HEAD_DIM:(h + 1) * HEAD_DIM]
            t = t * lax.rsqrt(jnp.mean(t * t, axis=0, keepdims=True) + EPS) * gain_ref[h * HEAD_DIM:(h + 1) * HEAD_DIM]
            x1, x2 = t[0:half], t[half:ROT_DIM]
            pieces += [x1 * cs - x2 * sn, x2 * cs + x1 * sn, t[ROT_DIM:]]
        return jnp.concatenate(pieces, axis=0)

    qt_ref[...] = (norm_rope(zt[:ATTN_W], gqc_ref) * LOG2_SCORE_SCALE).astype(qt_ref.dtype)
    kt = norm_rope(zt[ATTN_W:2 * ATTN_W], gkc_ref)
    kt_ref[...] = kt
    k = kt.T
    kb_ref[...] = k.astype(kb_ref.dtype)
    for b in range(tm // MOBA_BLOCK):
        kmean_ref[0, b:b + 1, :] = jnp.mean(k[b * MOBA_BLOCK:(b + 1) * MOBA_BLOCK], axis=0, keepdims=True)
    vt = zt[2 * ATTN_W:]
    vt_ref[...] = vt
    ones = jnp.ones((V_AUG_ROWS - HEAD_DIM, MOBA_BLOCK), vtb_ref.dtype)
    for b in range(tm // MOBA_BLOCK):
        for h in range(ATTN_HEADS):
            vtb_ref[b, h, :HEAD_DIM, :] = vt[h * HEAD_DIM:(h + 1) * HEAD_DIM,
                                             b * MOBA_BLOCK:(b + 1) * MOBA_BLOCK].astype(vtb_ref.dtype)
            vtb_ref[b, h, HEAD_DIM:, :] = ones

    def proj(c):
        return _mm(xn, w_ref[:, c * CONV_W:(c + 1) * CONV_W])

    bg = proj(0)
    u = proj(1) * proj(2)

    @pl.when(i == 0)
    def _():
        ubuf[0:8, :] = jnp.zeros((8, CONV_W), F32)

    ubuf[8:8 + tm, :] = u
    conv = _conv_taps(ubuf[6:6 + tm, :], ubuf[7:7 + tm, :], u, cw_ref[...])
    ubuf[0:8, :] = u[tm - 8:tm, :]
    tail_ref[...] = u[tm - 8:tm, :]
    cm_ref[...] = (bg * conv).astype(cm_ref.dtype)


def _mix_in_seq(x, gain, w_rows, w_t, gq_col, gk_col, inv8, conv_w):
    t = x.shape[0]
    tm = min(ROW_TILE, t)
    assert t % tm == 0 and tm % MOBA_BLOCK == 0
    n, nb = t // tm, tm // MOBA_BLOCK
    f32t = jax.ShapeDtypeStruct((ATTN_W, t), F32)
    b16o = jax.ShapeDtypeStruct((t, ATTN_W), MXU_DTYPE)
    cols = pl.BlockSpec((ATTN_W, tm), lambda i: (0, i))
    return pl.pallas_call(
        functools.partial(_mix_in_seq_body, tm), grid=(n,),
        in_specs=[_rows(tm, D_MODEL), _full((1, D_MODEL)), _full(w_rows.shape), _full(w_t.shape),
                  _full((ATTN_W, 1)), _full((ATTN_W, 1)), _full((ROT_DIM // 2, 1)), _full((CONV_WIDTH, CONV_W))],
        out_specs=[cols, cols, _rows(tm, ATTN_W), cols,
                   pl.BlockSpec((nb, ATTN_HEADS, V_AUG_ROWS, MOBA_BLOCK), lambda i: (i, 0, 0, 0)), _rows(tm, CONV_W),
                   pl.BlockSpec((1, nb, ATTN_W), lambda i: (i, 0, 0)), _full((8, CONV_W))],
        out_shape=[jax.ShapeDtypeStruct((ATTN_W, t), MXU_DTYPE), f32t, b16o, f32t,
                   jax.ShapeDtypeStruct((t // MOBA_BLOCK, ATTN_HEADS, V_AUG_ROWS, MOBA_BLOCK), MXU_DTYPE), b16o,
                   jax.ShapeDtypeStruct((n, nb, ATTN_W), F32), jax.ShapeDtypeStruct((8, CONV_W), F32)],
        scratch_shapes=[pltpu.VMEM((tm + 8, CONV_W), F32)],
        compiler_params=_cparams("arbitrary"), name="mix_in_seq",
    )(x, gain, w_rows, w_t, gq_col, gk_col, inv8, conv_w)


def _mix_in_rows_body(pos, x_ref, g_ref, w_ref, gq_ref, gk_ref, hm_ref, inv_ref, cw_ref, p2_ref, p1_ref,
                      qs_ref, k_ref, v_ref, cm_ref, u_ref):
    xn = _rms(x_ref[...], g_ref[...]).astype(MXU_DTYPE)

    def proj(c):
        return _mm(xn, w_ref[:, c * ATTN_W:(c + 1) * ATTN_W])

    posf = jnp.full((x_ref.shape[0], V7X_LANES), pos, F32)
    q = _rope_rows(_head_rms_rows(proj(0), gq_ref[...], hm_ref[...]), posf, inv_ref[...])
    k = _rope_rows(_head_rms_rows(proj(1), gk_ref[...], hm_ref[...]), posf, inv_ref[...])
    qs_ref[...] = q * (HEAD_DIM ** -0.5)
    k_ref[...] = k
    v_ref[...] = proj(2)
    bg = proj(3)
    u = proj(4) * proj(5)
    u_ref[...] = u
    cm_ref[...] = (bg * _conv_taps(p2_ref[...], p1_ref[...], u, cw_ref[...])).astype(cm_ref.dtype)


def _mix_in_rows(x, gain, w_in, gq, gk, hmat, inv_lanes, conv_w, prev2, prev1, *, pos):
    t = x.shape[0]
    f32o = jax.ShapeDtypeStruct((t, ATTN_W), F32)
    blk = _rows(t, ATTN_W)
    return pl.pallas_call(
        functools.partial(_mix_in_rows_body, pos), grid=(1,),
        in_specs=[_rows(t, D_MODEL), _full((1, D_MODEL)), _full(w_in.shape), _full((1, ATTN_W)), _full((1, ATTN_W)),
                  _full(hmat.shape), _full((1, V7X_LANES)), _full((CONV_WIDTH, CONV_W)), blk, blk],
        out_specs=[blk] * 5,
        out_shape=[f32o, f32o, f32o, jax.ShapeDtypeStruct((t, ATTN_W), MXU_DTYPE), f32o],
        compiler_params=_cparams("arbitrary"), name="mix_in_rows",
    )(x, gain, w_in, gq, gk, hmat, inv_lanes, conv_w, prev2, prev1)


def _moba_seq_body(qt_ref, k_ref, vt_ref, km_ref, o_ref, m_scr, acc_scr, id_scr, s_even, s_odd):
    for sub in range(MOBA_QUERY_BLOCKS_PER_STEP):
        _moba_query_block(pl.program_id(1) * MOBA_QUERY_BLOCKS_PER_STEP + sub,
                          qt_ref[:, sub * MOBA_BLOCK:(sub + 1) * MOBA_BLOCK], k_ref, vt_ref, km_ref,
                          o_ref.at[sub * MOBA_BLOCK:(sub + 1) * MOBA_BLOCK], m_scr, acc_scr, id_scr, s_even, s_odd)


def _moba_query_block(i, qt, k_ref, vt_ref, km_ref, o_ref, m_scr, acc_scr, id_scr, s_even, s_odd):
    blk = MOBA_BLOCK
    last = vt_ref.shape[0] - 1
    frow = lax.broadcasted_iota(jnp.int32, (V7X_LANES, blk), 0)
    qh = [jnp.where((frow < HEAD_DIM) == (h == 0), qt, jnp.zeros_like(qt)) for h in range(2)]
    brow = frow.astype(F32)
    krow = lax.broadcasted_iota(jnp.int32, (blk, blk), 0)
    qcol = lax.broadcasted_iota(jnp.int32, (blk, blk), 1)

    def keys(j):
        return k_ref[pl.ds(pl.multiple_of(j * blk, blk), blk), :]

    def raw_scores(dst, step):
        for d in range(2):
            kd = keys(jnp.minimum(2 * step + d, last))
            for h in range(2):
                dst[h, d] = _mm(kd, qh[h])

    def consume(src, step):
        j0 = 2 * step
        for h in range(2):
            vs = jnp.concatenate([vt_ref[jnp.minimum(j0 + d, last), h] for d in range(2)], axis=1)
            ss, picked = [], []
            for d in range(2):
                jf = (j0 + d).astype(F32)
                picked.append((id_scr[h, 0:1, :] == jf) | (id_scr[h, 1:2, :] == jf) | (id_scr[h, 2:3, :] == jf))
                ss.append(src[h, d])
            m_prev = m_scr[h][0:1]
            m_new = m_prev
            for d in range(2):
                m_new = jnp.maximum(m_new, jnp.where(picked[d], jnp.max(ss[d], axis=0, keepdims=True), NEG))
            p = jnp.concatenate([jnp.exp2(ss[d] - jnp.where(picked[d], m_new, -NEG)) for d in range(2)],
                                axis=0).astype(MXU_DTYPE)
            acc_scr[h] = jnp.exp2(m_prev - m_new) * acc_scr[h] + _mm(vs, p)
            m_scr[h] = jnp.broadcast_to(m_new, (8, blk))

    k_own = keys(i)
    for h in range(2):
        gate = jnp.where(frow < i, _mm(km_ref[...], qh[h]), NEG)
        for r in range(MOBA_TOPK):
            mx = jnp.max(gate, axis=0, keepdims=True)
            idx = jnp.min(jnp.where(gate == mx, brow, float(V7X_LANES)), axis=0, keepdims=True)
            id_scr[h, r:r + 1, :] = jnp.where(mx > 0.5 * NEG, idx, -1.0)
            gate = jnp.where(brow == idx, NEG, gate)
        s = jnp.where(krow <= qcol, _mm(k_own, qh[h]), NEG)
        m = jnp.max(s, axis=0, keepdims=True)
        m_scr[h] = jnp.broadcast_to(m, (8, blk))
        acc_scr[h] = _mm(vt_ref[i, h], jnp.exp2(s - m).astype(MXU_DTYPE))

    raw_scores(s_even, 0)

    def two_steps(t, carry):
        raw_scores(s_odd, 2 * t + 1)
        consume(s_even, 2 * t)
        raw_scores(s_even, 2 * t + 2)
        consume(s_odd, 2 * t + 1)
        return carry

    lax.fori_loop(0, (i + 3) // 4, two_steps, 0)
    ot = jnp.concatenate([acc_scr[h][:HEAD_DIM] / acc_scr[h][HEAD_DIM:HEAD_DIM + 1] for h in range(2)], axis=0)
    o_ref[...] = ot.T.astype(o_ref.dtype)


def _moba_seq(qt, kb, vtb, kmean_pad):
    t = kb.shape[0]
    nblk = t // MOBA_BLOCK
    per = MOBA_QUERY_BLOCKS_PER_STEP
    assert t % (per * MOBA_BLOCK) == 0 and nblk <= V7X_LANES
    pairs = ATTN_W // V7X_LANES
    return pl.pallas_call(
        _moba_seq_body,
        grid=(pairs, nblk // per),
        in_specs=[pl.BlockSpec((V7X_LANES, per * MOBA_BLOCK), lambda p, i: (p, i)),
                  pl.BlockSpec((t, V7X_LANES), lambda p, i: (0, p)),
                  pl.BlockSpec((nblk, 2, V_AUG_ROWS, MOBA_BLOCK), lambda p, i: (0, p, 0, 0)),
                  pl.BlockSpec((V7X_LANES, V7X_LANES), lambda p, i: (0, p))],
        out_specs=pl.BlockSpec((per * MOBA_BLOCK, V7X_LANES), lambda p, i: (i, p)),
        out_shape=jax.ShapeDtypeStruct((t, ATTN_W), MXU_DTYPE),
        scratch_shapes=[pltpu.VMEM((2, 8, MOBA_BLOCK), F32), pltpu.VMEM((2, V_AUG_ROWS, MOBA_BLOCK), F32),
                        pltpu.VMEM((2, 8, MOBA_BLOCK), F32)] + [pltpu.VMEM((2, 2, MOBA_BLOCK, MOBA_BLOCK), F32)] * 2,
        compiler_params=_cparams("arbitrary", "arbitrary"), name="moba_seq")(qt, kb, vtb, kmean_pad)


def _head_rows(q_row):
    sub = lax.broadcasted_iota(jnp.int32, (ATTN_HEADS, ATTN_W), 0)
    lane = lax.broadcasted_iota(jnp.int32, (ATTN_HEADS, ATTN_W), 1)
    return jnp.where(lane // HEAD_DIM == sub, jnp.broadcast_to(q_row, (ATTN_HEADS, ATTN_W)), 0.0)


def _moba_paged_scores_body(n_pages, pt_ref, q_ref, kn_ref, k_hbm, p_ref, ids_ref, pn_ref, kbuf, sem, gsum):
    per = SCORE_PAGES_PER_STEP
    groups = n_pages // per
    t = pl.program_id(0)
    n_steps = pl.num_programs(0)
    grp = t % groups
    slot = t % SCORE_RING

    def page_copy(step, g):
        buf = step % SCORE_RING
        return pltpu.make_async_copy(k_hbm.at[pt_ref[step * per + g]], kbuf.at[buf, g], sem.at[buf])

    def start_step(step):
        @pl.when(step < n_steps)
        def _():
            for g in range(per):
                page_copy(step, g).start()

    @pl.when(t == 0)
    def _():
        for ahead in range(SCORE_RING - 1):
            start_step(t + ahead)

    start_step(t + SCORE_RING - 1)
    for g in range(per):
        page_copy(t, g).wait()

    qd = _head_rows(q_ref[0])
    qb = qd.astype(MXU_DTYPE)
    lane = lax.broadcasted_iota(jnp.int32, (ATTN_HEADS, V7X_LANES), 1)

    @pl.when(grp == 0)
    def _():
        gsum[...] = jnp.zeros_like(gsum)

    for g in range(per):
        pg = grp * per + g
        s = _mm(qb, kbuf[slot, g].astype(MXU_DTYPE))
        p_ref[0, pg] = s
        gsum[...] += jnp.where(lane == pg // PAGES_PER_BLOCK, jnp.sum(s, axis=1, keepdims=True), 0.0)

    @pl.when(grp == groups - 1)
    def _():
        n_blocks = n_pages // PAGES_PER_BLOCK
        lanef = lane.astype(F32)
        gate = jnp.where(lane < n_blocks, gsum[...] * (1.0 / MOBA_BLOCK), NEG)
        ids = _top_ids(gate, lanef)
        for r in range(MOBA_TOPK):
            ids_ref[0, r] = jnp.broadcast_to(ids[r], (ATTN_HEADS, V7X_LANES)).astype(jnp.int32)
        s_new = jnp.sum(qd * kn_ref[0], axis=1, keepdims=True)
        sc = p_ref[0]
        blk = (lax.broadcasted_iota(jnp.int32, sc.shape, 0) // PAGES_PER_BLOCK).astype(F32)
        picked = (blk == ids[0][None]) | (blk == ids[1][None]) | (blk == ids[2][None])
        sc = jnp.where(picked, sc, NEG)
        m = jnp.maximum(jnp.max(jnp.max(sc, axis=0), axis=1, keepdims=True), s_new)
        e = jnp.where(picked, jnp.exp(sc - m[None]), 0.0)
        e_new = jnp.exp(s_new - m)
        inv = 1.0 / (jnp.sum(jnp.sum(e, axis=0), axis=1, keepdims=True) + e_new)
        p_ref[0] = e * inv[None]
        pn_ref[0] = jnp.broadcast_to(e_new * inv, (ATTN_HEADS, V7X_LANES))


def _moba_paged_pv_body(n_pages, pt_ref, ids_ref, p_ref, pn_ref, vn_ref, v_hbm, o_ref, vbuf, sem):
    b = pl.program_id(0)
    slot = b % 2
    n_slots = ATTN_HEADS * MOBA_TOPK

    def first_page(seq, s):
        return jnp.maximum(ids_ref[seq * n_slots + s], 0) * PAGES_PER_BLOCK

    def slab_copy(seq, buf, h, r, half):
        page = pt_ref[seq * n_pages + first_page(seq, h * MOBA_TOPK + r) + half]
        rows = pl.ds(h * HEAD_DIM, HEAD_DIM)
        return pltpu.make_async_copy(v_hbm.at[page, rows], vbuf.at[buf, r * PAGES_PER_BLOCK + half, rows],
                                     sem.at[buf])

    def for_all_slabs(fn):
        for h in range(ATTN_HEADS):
            for r in range(MOBA_TOPK):
                for half in range(PAGES_PER_BLOCK):
                    fn(h, r, half)

    @pl.when(b == 0)
    def _():
        for_all_slabs(lambda h, r, half: slab_copy(b, slot, h, r, half).start(priority=half))

    @pl.when(b + 1 < pl.num_programs(0))
    def _():
        for_all_slabs(lambda h, r, half: slab_copy(b + 1, 1 - slot, h, r, half).start(priority=half))

    for_all_slabs(lambda h, r, half: slab_copy(b, slot, h, r, half).wait())

    sub = lax.broadcasted_iota(jnp.int32, (ATTN_HEADS, V7X_LANES), 0)
    full = pn_ref[0][:, 0:1] * vn_ref[0]
    for r in range(MOBA_TOPK):
        for half in range(PAGES_PER_BLOCK):
            w = jnp.zeros((ATTN_HEADS, V7X_LANES), F32)
            for h in range(ATTN_HEADS):
                w = jnp.where(sub == h, p_ref[0, first_page(b, h * MOBA_TOPK + r) + half], w)
            full = full + lax.dot_general(w.astype(MXU_DTYPE),
                                          vbuf[slot, r * PAGES_PER_BLOCK + half].astype(MXU_DTYPE), NT_DIMS,
                                          preferred_element_type=F32)
    sub_w = lax.broadcasted_iota(jnp.int32, (ATTN_HEADS, ATTN_W), 0)
    lane_w = lax.broadcasted_iota(jnp.int32, (ATTN_HEADS, ATTN_W), 1)
    o_ref[0] = jnp.sum(jnp.where(lane_w // HEAD_DIM == sub_w, full, 0.0), axis=0, keepdims=True).astype(o_ref.dtype)


def _moba_paged(qs, k_new, v_new, cache_k, cache_v, page_ids):
    nb, n_pages = page_ids.shape
    per = SCORE_PAGES_PER_STEP
    assert n_pages % PAGES_PER_BLOCK == 0 and n_pages // PAGES_PER_BLOCK <= V7X_LANES and n_pages % per == 0
    groups = n_pages // per
    pt = page_ids.reshape(-1)
    row3 = lambda a: a.reshape(nb, 1, ATTN_W)
    page_shape = (ATTN_W, PAGE_SIZE)

    def per_seq(steps_per_seq, *shape):
        return pl.BlockSpec((1,) + shape, lambda t, *_: (t // steps_per_seq,) + (0,) * len(shape))

    probs, ids, p_new = pl.pallas_call(
        functools.partial(_moba_paged_scores_body, n_pages),
        grid_spec=pltpu.PrefetchScalarGridSpec(
            num_scalar_prefetch=1, grid=(nb * groups,),
            in_specs=[per_seq(groups, 1, ATTN_W), per_seq(groups, 1, ATTN_W), pl.BlockSpec(memory_space=pl.ANY)],
            out_specs=[per_seq(groups, n_pages, ATTN_HEADS, V7X_LANES),
                       per_seq(groups, MOBA_TOPK, ATTN_HEADS, V7X_LANES), per_seq(groups, ATTN_HEADS, V7X_LANES)],
            scratch_shapes=[pltpu.VMEM((SCORE_RING, per) + page_shape, F32), pltpu.SemaphoreType.DMA((SCORE_RING,)),
                            pltpu.VMEM((ATTN_HEADS, V7X_LANES), F32)]),
        out_shape=[jax.ShapeDtypeStruct((nb, n_pages, ATTN_HEADS, V7X_LANES), F32),
                   jax.ShapeDtypeStruct((nb, MOBA_TOPK, ATTN_HEADS, V7X_LANES), jnp.int32),
                   jax.ShapeDtypeStruct((nb, ATTN_HEADS, V7X_LANES), F32)],
        compiler_params=_cparams("arbitrary"), name="moba_paged_scores",
    )(pt, row3(qs), row3(k_new), cache_k)
    slot_ids = jnp.transpose(ids[:, :, :, 0], (0, 2, 1)).reshape(-1)
    return pl.pallas_call(
        functools.partial(_moba_paged_pv_body, n_pages),
        grid_spec=pltpu.PrefetchScalarGridSpec(
            num_scalar_prefetch=2, grid=(nb,),
            in_specs=[per_seq(1, n_pages, ATTN_HEADS, V7X_LANES), per_seq(1, ATTN_HEADS, V7X_LANES),
                      per_seq(1, 1, ATTN_W), pl.BlockSpec(memory_space=pl.ANY)],
            out_specs=per_seq(1, 1, ATTN_W),
            scratch_shapes=[pltpu.VMEM((2, MOBA_TOPK * PAGES_PER_BLOCK) + page_shape, F32),
                            pltpu.SemaphoreType.DMA((2,))]),
        out_shape=jax.ShapeDtypeStruct((nb, 1, ATTN_W), MXU_DTYPE),
        compiler_params=_cparams("arbitrary"), name="moba_paged_pv",
    )(pt, slot_ids, probs, p_new, row3(v_new), cache_v).reshape(nb, ATTN_W)


def _gmlp_in_body(seq_mode, tm, *refs):
    if seq_mode:
        x_ref, g_ref, w_ref, lg_ref, lb_ref, ws_ref, bias_ref, act_ref = refs
    else:
        x_ref, g_ref, w_ref, lg_ref, lb_ref, wrow_ref, brow_ref, act_ref, v_ref = refs
    xn = _rms(x_ref[...], g_ref[...]).astype(MXU_DTYPE)
    u = jax.nn.gelu(_mm(xn, w_ref[:, :GMLP_W]))
    v = jax.nn.gelu(_mm(xn, w_ref[:, GMLP_W:]))
    vc = v - jnp.mean(v, axis=-1, keepdims=True)
    v = vc * lax.rsqrt(jnp.mean(vc * vc, axis=-1, keepdims=True) + EPS) * lg_ref[...] + lb_ref[...]
    if not seq_mode:
        v_ref[...] = v
        act_ref[...] = (u * (v * wrow_ref[...] + brow_ref[...])).astype(act_ref.dtype)
        return
    dg = GMLP_W // GMLP_GROUPS
    r = lax.broadcasted_iota(jnp.int32, (GMLP_CHUNK, GMLP_CHUNK), 0)
    c = lax.broadcasted_iota(jnp.int32, (GMLP_CHUNK, GMLP_CHUNK), 1)
    vb = v.astype(MXU_DTYPE)
    for g in range(GMLP_GROUPS):
        ws = jnp.where(c <= r, ws_ref[g], 0.0).astype(MXU_DTYPE)
        cols = slice(g * dg, (g + 1) * dg)
        for ch in range(tm // GMLP_CHUNK):
            rows = slice(ch * GMLP_CHUNK, (ch + 1) * GMLP_CHUNK)
            s = _mm(ws, vb[rows, cols]) + bias_ref[:, cols]
            act_ref[rows, cols] = (u[rows, cols] * s).astype(act_ref.dtype)


def _gmlp_in(x, gain, w_in, ln_g, ln_b, *, seq_mode, w_s=None, bias=None, wrow=None, brow=None):
    t = x.shape[0]
    tm = min(ROW_TILE, t)
    assert t % tm == 0 and (not seq_mode or tm % GMLP_CHUNK == 0)
    vec = _full((1, GMLP_W))
    in_specs = [_rows(tm, D_MODEL), vec, _full(w_in.shape), vec, vec]
    act = jax.ShapeDtypeStruct((t, GMLP_W), MXU_DTYPE)
    if seq_mode:
        in_specs += [_full(w_s.shape), _full(bias.shape)]
        args = [x, gain, w_in, ln_g, ln_b, w_s, bias]
        out_shape, out_specs = act, _rows(tm, GMLP_W)
    else:
        in_specs += [vec, vec]
        args = [x, gain, w_in, ln_g, ln_b, wrow, brow]
        out_shape = [act, jax.ShapeDtypeStruct((t, GMLP_W), F32)]
        out_specs = [_rows(tm, GMLP_W)] * 2
    return pl.pallas_call(
        functools.partial(_gmlp_in_body, seq_mode, tm), grid=(t // tm,), in_specs=in_specs,
        out_specs=out_specs, out_shape=out_shape, compiler_params=_cparams("arbitrary"),
        name="gmlp_in_seq" if seq_mode else "gmlp_in_rows")(*args)


def _mem_kv_body(m_ref, g_ref, wk_ref, wv_ref, gk_ref, k_ref, v_ref, kb_ref, vb_ref):
    mm = _rms(m_ref[...], g_ref[...]).astype(MXU_DTYPE)
    k = _mm(mm, wk_ref[...])
    v = _mm(mm, wv_ref[...])
    k = jnp.concatenate(
        [_rms(k[:, h * MEM_HEAD_DIM:(h + 1) * MEM_HEAD_DIM], gk_ref[...]) for h in range(MEM_HEADS)], axis=1)
    k_ref[...] = k
    v_ref[...] = v
    kb_ref[...] = k.astype(kb_ref.dtype)
    vb_ref[...] = v.astype(vb_ref.dtype)


def _mem_kv(mem, gain, w_k, w_v, g_k):
    m = mem.shape[0]
    f32o = jax.ShapeDtypeStruct((m, D_MODEL), F32)
    b16o = jax.ShapeDtypeStruct((m, D_MODEL), MXU_DTYPE)
    blk = _full((m, D_MODEL))
    return pl.pallas_call(
        _mem_kv_body, grid=(1,),
        in_specs=[blk, _full((1, D_MODEL)), _full(w_k.shape), _full(w_v.shape), _full((1, MEM_HEAD_DIM))],
        out_specs=[blk] * 4, out_shape=[f32o, f32o, b16o, b16o],
        compiler_params=_cparams("arbitrary"), name="mem_kv")(mem, gain, w_k, w_v, g_k)


def _post_mix_body(n_act, shared_mem, *refs):
    h_ref = refs[0]
    act_refs = refs[1:1 + n_act]
    w_refs = refs[1 + n_act:1 + 2 * n_act]
    rest = refs[1 + 2 * n_act:]
    if shared_mem:
        g_ref, wq_ref, gq_ref, mk_ref, mv_ref, wo_ref, o_ref = rest
    else:
        g_ref, wq_ref, gq_ref, h1_ref, q_ref = rest
    h1 = h_ref[...]
    for a_ref, w_ref in zip(act_refs, w_refs):
        h1 = h1 + _mm(a_ref[...], w_ref[...])
    qc = _mm(_rms(h1, g_ref[...]).astype(MXU_DTYPE), wq_ref[...])
    heads = []
    for hd in range(MEM_HEADS):
        cols = slice(hd * MEM_HEAD_DIM, (hd + 1) * MEM_HEAD_DIM)
        qn = _rms(qc[:, cols], gq_ref[...])
        if not shared_mem:
            heads.append(qn)
            continue
        s = lax.dot_general(qn.astype(MXU_DTYPE), mk_ref[:, cols], NT_DIMS,
                            preferred_element_type=F32) * (MEM_HEAD_DIM ** -0.5)
        p = jnp.exp(s - jnp.max(s, axis=1, keepdims=True))
        o = _mm(p.astype(MXU_DTYPE), mv_ref[:, cols]) / jnp.sum(p, axis=1, keepdims=True)
        heads.append(o.astype(MXU_DTYPE))
    cat = jnp.concatenate(heads, axis=1)
    if shared_mem:
        o_ref[...] = h1 + _mm(cat, wo_ref[...])
    else:
        h1_ref[...] = h1
        q_ref[...] = cat


def _post_mix(h, acts, w_outs, gain, w_q, g_q, mem=None, w_o=None):
    t = h.shape[0]
    tm = min(ROW_TILE, t)
    assert t % tm == 0
    shared = mem is not None
    in_specs = [_rows(tm, D_MODEL)] + [_rows(tm, a.shape[1]) for a in acts] + [_full(w.shape) for w in w_outs]
    in_specs += [_full((1, D_MODEL)), _full(w_q.shape), _full((1, MEM_HEAD_DIM))]
    args = [h, *acts, *w_outs, gain, w_q, g_q]
    f32o = jax.ShapeDtypeStruct((t, D_MODEL), F32)
    if shared:
        in_specs += [_full(mem[0].shape), _full(mem[1].shape), _full(w_o.shape)]
        args += [mem[0], mem[1], w_o]
        out_shape, out_specs = f32o, _rows(tm, D_MODEL)
    else:
        out_shape, out_specs = [f32o, f32o], [_rows(tm, D_MODEL)] * 2
    return pl.pallas_call(
        functools.partial(_post_mix_body, len(acts), shared), grid=(t // tm,), in_specs=in_specs,
        out_specs=out_specs, out_shape=out_shape, compiler_params=_cparams("arbitrary"),
        name="post_mix_shared" if shared else "post_mix_rows")(*args)


def _mem_chunks(a):
    lead = a.shape[:-2]
    a = a.reshape(lead + (MEM_HEADS, MEM_HEAD_DIM // V7X_LANES, V7X_LANES))
    return jnp.swapaxes(a, -3, -2).reshape(lead + (MEM_HEADS * MEM_HEAD_DIM // V7X_LANES, V7X_LANES))


def _mem_unchunk(a):
    lead = a.shape[:-2]
    a = a.reshape(lead + (MEM_HEAD_DIM // V7X_LANES, MEM_HEADS, V7X_LANES))
    return jnp.swapaxes(a, -3, -2).reshape(lead + (D_MODEL,))


def _mem_attend_rows_body(q_ref, k_ref, v_ref, o_ref):
    t = k_ref[0] * q_ref[0][None]
    t = t + pltpu.roll(t, MEM_HEADS, 1)
    s = jnp.sum(t, axis=2, keepdims=True) * (MEM_HEAD_DIM ** -0.5)
    p = jnp.exp(s - jnp.max(s, axis=0, keepdims=True))
    o = jnp.sum(p * v_ref[0], axis=0) / jnp.sum(p, axis=0)
    o_ref[0] = o.astype(o_ref.dtype)


def _mem_attend_rows(qn, mem_k, mem_v, first):
    nb, m = qn.shape[0], mem_k.shape[1]
    chunks = mem_k.shape[2:]
    row = pl.BlockSpec((1,) + chunks, lambda b: (b, 0, 0))
    memb = pl.BlockSpec((1, m) + chunks, lambda b: (first + b, 0, 0, 0))
    q8 = _mem_chunks(qn.reshape(nb, MEM_HEADS, MEM_HEAD_DIM))
    o8 = pl.pallas_call(
        _mem_attend_rows_body, grid=(nb,), in_specs=[row, memb, memb], out_specs=row,
        out_shape=jax.ShapeDtypeStruct((nb,) + chunks, F32),
        compiler_params=_cparams("arbitrary"), name="mem_attend_rows")(q8, mem_k, mem_v)
    return _mem_unchunk(o8).astype(MXU_DTYPE)


def _ffn_body(pre_proj, *refs):
    if pre_proj:
        h_ref, o_in_ref, wo_ref, g_ref, w1_ref, w2_ref, out_ref, h_scr, xn_scr, acc = refs
    else:
        h_ref, g_ref, w1_ref, w2_ref, out_ref, h_scr, xn_scr, acc = refs
    f = pl.program_id(1)

    @pl.when(f == 0)
    def _():
        h = h_ref[...]
        if pre_proj:
            h = h + _mm(o_in_ref[...], wo_ref[...])
        h_scr[...] = h
        xn_scr[...] = _rms(h, g_ref[...]).astype(xn_scr.dtype)
        acc[...] = jnp.zeros_like(acc)

    a = jnp.maximum(_mm(xn_scr[...], w1_ref[...]), 0.0)
    acc[...] += _mm((a * a).astype(MXU_DTYPE), w2_ref[...])

    @pl.when(f == pl.num_programs(1) - 1)
    def _():
        out_ref[...] = h_scr[...] + acc[...]


def _ffn(h, gain, w1, w2, pre=None):
    t = h.shape[0]
    tm = min(FFN_ROW_TILE, t)
    tf = FFN_COL_TILE
    assert t % tm == 0 and FFN_W % tf == 0
    rows = pl.BlockSpec((tm, D_MODEL), lambda i, f: (i, 0))
    in_specs, args = [rows], [h]
    if pre is not None:
        in_specs += [rows, pl.BlockSpec(pre[1].shape, lambda i, f: (0, 0))]
        args += list(pre)
    in_specs += [pl.BlockSpec((1, D_MODEL), lambda i, f: (0, 0)),
                 pl.BlockSpec((D_MODEL, tf), lambda i, f: (0, f)), pl.BlockSpec((tf, D_MODEL), lambda i, f: (f, 0))]
    args += [gain, w1, w2]
    return pl.pallas_call(
        functools.partial(_ffn_body, pre is not None), grid=(t // tm, FFN_W // tf), in_specs=in_specs,
        out_specs=rows, out_shape=jax.ShapeDtypeStruct((t, D_MODEL), F32),
        scratch_shapes=[pltpu.VMEM((tm, D_MODEL), F32), pltpu.VMEM((tm, D_MODEL), MXU_DTYPE),
                        pltpu.VMEM((tm, D_MODEL), F32)],
        compiler_params=_cparams("arbitrary", "arbitrary"), name="ffn")(*args)


def kernel(x_prompt, x_sample, mem_prompt, cache_attn_k, cache_attn_v, state_conv, cache_mem_k, cache_mem_v,
           page_table, norm_mix, norm_cross, norm_mem, norm_ffn, mix_w_in, attn_g_q, attn_g_k, conv_w, mix_w_out,
           gmlp_w_in, gmlp_ln_g, gmlp_ln_b, gmlp_w_s, gmlp_b_s, gmlp_w_out, cross_w_q, cross_w_k, cross_w_v,
           cross_w_o, cross_g_q, cross_g_k, ffn_w1, ffn_w2):
    b_p, t_p, _ = x_prompt.shape
    b_s, t_s, _ = x_sample.shape
    assert b_p == 1 and t_s == 1
    depth = norm_mix.shape[0]
    n_pages = page_table.shape[1]
    past_len = n_pages * PAGE_SIZE
    bf = lambda a: a.astype(MXU_DTYPE)
    vec = lambda a: a.reshape(1, -1)

    half = ROT_DIM // 2
    inv = jnp.power(jnp.float32(ROPE_THETA), -jnp.arange(half, dtype=jnp.float32) * (2.0 / ROT_DIM))
    lane = jnp.arange(V7X_LANES) % HEAD_DIM
    inv_lanes = jnp.where(lane < ROT_DIM, inv[lane % half], 0.0).reshape(1, V7X_LANES).astype(F32)
    inv8 = inv.reshape(half, 1)
    hmat = bf(jnp.kron(jnp.eye(ATTN_HEADS, dtype=F32), jnp.ones((HEAD_DIM, HEAD_DIM), F32)))

    hp = x_prompt.reshape(t_p, D_MODEL)
    hs = x_sample.reshape(b_s, D_MODEL)
    ak_p, av_p, ak_s, av_s, cs_p, cs_s, gv_s, mk_p, mv_p = [], [], [], [], [], [], [], [], []
    for layer in range(depth):
        li = layer // 2
        gmix = vec(norm_mix[layer])
        if layer % 2 == 0:
            w_in = bf(mix_w_in[li])
            gq = jnp.tile(attn_g_q[li], ATTN_HEADS)
            gk = jnp.tile(attn_g_k[li], ATTN_HEADS)
            w_out = bf(mix_w_out[li])
            w_outs = [w_out[:ATTN_W], w_out[ATTN_W:]]
            qt, kt, kb, vt, vtb, cmix, kmean, tail = _mix_in_seq(
                hp, gmix, w_in[:, 3 * ATTN_W:], w_in[:, :3 * ATTN_W].T, gq.reshape(ATTN_W, 1), gk.reshape(ATTN_W, 1),
                inv8, conv_w[li])
            kmean = kmean.reshape(-1, ATTN_W)
            kmean_pad = bf(jnp.pad(kmean, ((0, V7X_LANES - kmean.shape[0]), (0, 0))))
            acts_p = [_moba_seq(qt, kb, vtb, kmean_pad), cmix]
            as_cache = lambda a: jnp.transpose(a.reshape(ATTN_HEADS, HEAD_DIM, t_p), (2, 0, 1))[None]
            ak_p.append(as_cache(kt))
            av_p.append(as_cache(vt))
            cs_p.append(tail[8 - (CONV_WIDTH - 1):].reshape(b_p, CONV_WIDTH - 1, CONV_W))

            qs_s, k_s, v_s, cmix_s, u_s = _mix_in_rows(
                hs, gmix, w_in, vec(gq), vec(gk), hmat, inv_lanes, conv_w[li], state_conv[li, :, 0], state_conv[li, :, 1],
                pos=past_len)
            pool = cache_attn_k.shape[1]
            paged = lambda c: jnp.transpose(c, (0, 1, 3, 4, 2)).reshape(-1, ATTN_W, PAGE_SIZE)
            attn_s = _moba_paged(qs_s, k_s, v_s, paged(cache_attn_k), paged(cache_attn_v), page_table + li * pool)
            acts_s = [attn_s, cmix_s]
            ak_s.append(k_s.reshape(b_s, t_s, ATTN_HEADS, HEAD_DIM))
            av_s.append(v_s.reshape(b_s, t_s, ATTN_HEADS, HEAD_DIM))
            cs_s.append(jnp.stack([state_conv[li, :, 1], u_s], axis=1))
        else:
            w_in = bf(gmlp_w_in[li])
            w_outs = [bf(gmlp_w_out[li])]
            lg, lb = vec(gmlp_ln_g[li]), vec(gmlp_ln_b[li])
            dg = GMLP_W // GMLP_GROUPS
            bias = jnp.repeat(gmlp_b_s[li].T, dg, axis=1)
            acts_p = [_gmlp_in(hp, gmix, w_in, lg, lb, seq_mode=True, w_s=gmlp_w_s[li], bias=bias)]
            wrow = vec(jnp.repeat(gmlp_w_s[li][:, 0, 0], dg))
            brow = vec(jnp.repeat(gmlp_b_s[li][:, 0], dg))
            act_s, gv = _gmlp_in(hs, gmix, w_in, lg, lb, seq_mode=False, wrow=wrow, brow=brow)
            acts_s = [act_s]
            gv_s.append(gv.reshape(b_s, t_s, GMLP_W))

        gcross, w_q, g_q = vec(norm_cross[layer]), bf(cross_w_q[layer]), vec(cross_g_q[layer])
        w_o = bf(cross_w_o[layer])
        mk, mv, mkb, mvb = _mem_kv(mem_prompt.reshape(-1, D_MODEL), vec(norm_mem[layer]), bf(cross_w_k[layer]),
                                   bf(cross_w_v[layer]), vec(cross_g_k[layer]))
        mk_p.append(mk.reshape(b_p, -1, MEM_HEADS, MEM_HEAD_DIM))
        mv_p.append(mv.reshape(b_p, -1, MEM_HEADS, MEM_HEAD_DIM))
        gffn, w1, w2 = vec(norm_ffn[layer]), bf(ffn_w1[layer]), bf(ffn_w2[layer])

        hp = _post_mix(hp, acts_p, w_outs, gcross, w_q, g_q, mem=(mkb, mvb), w_o=w_o)
        hp = _ffn(hp, gffn, w1, w2)

        h1_s, qn_s = _post_mix(hs, acts_s, w_outs, gcross, w_q, g_q)
        stored = lambda c: _mem_chunks(c).reshape((-1,) + c.shape[2:3] + (D_MODEL // V7X_LANES, V7X_LANES))
        o_s = _mem_attend_rows(qn_s, stored(cache_mem_k), stored(cache_mem_v), layer * b_s)
        hs = _ffn(h1_s, gffn, w1, w2, pre=(o_s, w_o))

    return (hp.reshape(b_p, t_p, D_MODEL), hs.reshape(b_s, t_s, D_MODEL), jnp.stack(ak_p), jnp.stack(av_p),
            jnp.stack(ak_s), jnp.stack(av_s), jnp.stack(cs_p), jnp.stack(cs_s), jnp.stack(gv_s),
            jnp.stack(mk_p), jnp.stack(mv_p))
```
